```python
import math
import jax, jax.numpy as jnp
from jax import lax
import numpy as np

D_MODEL = 1024
BATCH = 2
SEQ = 8192
DEPTH = 2
DEC_BATCH = 16
DEC_SEQ = 64
PAST_LEN = 2048

CHUNK = 64
N_A = DEPTH // 2
N_B = DEPTH - N_A
D_FF = 2816
RWKV_HEAD = 64
RWKV_HEADS = D_MODEL // RWKV_HEAD
DECAY_LORA = 64
A_LORA = 64
GATE_LORA = 128
GN_EPS = 64e-5
N_HEADS_B = 8
HEAD_QK = D_MODEL // N_HEADS_B // 2
HEAD_V = 2 * HEAD_QK
QK_W = N_HEADS_B * 2 * HEAD_QK
V_W = N_HEADS_B * HEAD_V
ROT_DIM = HEAD_QK // 4
ROPE_THETA = 500000.0
ATTN_SCALE = HEAD_QK ** -0.5
Q_BLOCK = 128
NORM_EPS = 1e-6
NEG_INF = -1e30

kernel_name = "streaming_rwkv7_diffattn_yoco"

F32 = jnp.float32


def _rms(x, g):
    x32 = x.astype(F32)
    y = x32 * lax.rsqrt(jnp.mean(x32 * x32, axis=-1, keepdims=True) + NORM_EPS) * g.astype(F32)
    return y.astype(x.dtype)


def _modulate(h, shift, scale):
    return h * (1.0 + scale) + shift


def _swiglu(h, w_in, w_out):
    gu = h @ w_in
    gate, up = jnp.split(gu, 2, axis=-1)
    return (jax.nn.silu(gate) * up) @ w_out


def _rope(x, pos):
    half = ROT_DIM // 2
    inv = ROPE_THETA ** (-jnp.arange(half, dtype=F32) * 2.0 / ROT_DIM)
    ang = pos.astype(F32)[:, None] * inv[None, :]
    cos = jnp.cos(ang)[None, :, None, None, :]
    sin = jnp.sin(ang)[None, :, None, None, :]
    xf = x.astype(F32)
    x1 = xf[..., :half]
    x2 = xf[..., half:ROT_DIM]
    out = jnp.concatenate([x1 * cos - x2 * sin, x2 * cos + x1 * sin, xf[..., ROT_DIM:]], axis=-1)
    return out.astype(x.dtype)


def _rwkv_step(S, inp):
    r_t, w_t, k_t, v_t, kk_t, b_t = inp
    sa = jnp.einsum('bhvk,bhk->bhv', S, -kk_t)
    S = S * w_t[:, :, None, :] + sa[..., None] * b_t[:, :, None, :] + v_t[..., None] * k_t[:, :, None, :]
    y = jnp.einsum('bhvk,bhk->bhv', S, r_t)
    return S, y


def _rwkv7_time_mix(h, shift_row, S0, i, P):
    B, T, D = h.shape
    dt = h.dtype
    mu = P['rwkv_mu'][i]
    xprev = jnp.concatenate([shift_row.astype(dt), h[:, :-1]], axis=1)
    dx = xprev - h
    xr, xw, xk, xv, xa, xg = [h + dx * mu[j] for j in range(6)]
    W = P['rwkv_w_rkv'][i]
    r = xr @ W[0]
    k = xk @ W[1]
    v = xv @ W[2]
    z = (P['rwkv_w0'][i] + jnp.tanh(xw @ P['rwkv_w1'][i]) @ P['rwkv_w2'][i]).astype(F32)
    w_log = -jax.nn.softplus(-z) - 0.5
    decay = jnp.exp(-jnp.exp(w_log))
    a = jax.nn.sigmoid((P['rwkv_a0'][i] + (xa @ P['rwkv_a1'][i]) @ P['rwkv_a2'][i]).astype(F32))
    g = jax.nn.sigmoid(xg @ P['rwkv_g1'][i]) @ P['rwkv_g2'][i]

    def heads(t):
        return t.astype(F32).reshape(B, T, RWKV_HEADS, RWKV_HEAD)

    kk = heads(k * P['rwkv_k_k'][i])
    kk = kk / jnp.maximum(jnp.linalg.norm(kk, axis=-1, keepdims=True), 1e-12)
    k = k.astype(F32) * (1.0 + (a - 1.0) * P['rwkv_k_a'][i].astype(F32))
    r_h, k_h, v_h, w_h, a_h = heads(r), heads(k), heads(v), heads(decay), heads(a)

    def tm(t):
        return jnp.moveaxis(t, 1, 0)

    S_T, y = lax.scan(_rwkv_step, S0.astype(F32),
                      (tm(r_h), tm(w_h), tm(k_h), tm(v_h), tm(kk), tm(kk * a_h)))
    y = jnp.moveaxis(y, 0, 1)
    mean = jnp.mean(y, axis=-1, keepdims=True)
    var = jnp.mean(jnp.square(y - mean), axis=-1, keepdims=True)
    y = ((y - mean) * lax.rsqrt(var + GN_EPS)).reshape(B, T, D)
    y = y * P['rwkv_ln_w'][i].astype(F32) + P['rwkv_ln_b'][i].astype(F32)
    bonus = (jnp.sum(r_h * k_h * P['rwkv_r_k'][i].astype(F32), axis=-1, keepdims=True) * v_h).reshape(B, T, D)
    out = ((y + bonus) * g.astype(F32)).astype(dt) @ P['rwkv_w_o'][i]
    return out, S_T, h[:, -1:]


def _diff_attend(q, qpos, k, v, kpos, lam):
    s = jnp.einsum('bqhcd,bkhcd->bhcqk', q.astype(F32), k) * ATTN_SCALE
    visible = (kpos // CHUNK)[None, :] <= (qpos // CHUNK)[:, None]
    s = jnp.where(visible, s, NEG_INF)
    p = jax.nn.softmax(s, axis=-1)
    a = p[:, :, 0] - lam * p[:, :, 1]
    return jnp.einsum('bhqk,bkhd->bqhd', a, v)


def _attend(q, qpos, k, v, kpos, lam):
    B, T = q.shape[0], q.shape[1]
    if T <= Q_BLOCK:
        return _diff_attend(q, qpos, k, v, kpos, lam)
    nb = T // Q_BLOCK
    qb = jnp.moveaxis(q.reshape(B, nb, Q_BLOCK, N_HEADS_B, 2, HEAD_QK), 1, 0)
    pb = qpos.reshape(nb, Q_BLOCK)
    ob = lax.map(lambda args: _diff_attend(args[0], args[1], k, v, kpos, lam), (qb, pb))
    return jnp.moveaxis(ob, 0, 1).reshape(B, T, N_HEADS_B, HEAD_V)


def _diff_attn_layer(h, qpos, k_all, v_all, kpos, j, layer_idx, P):
    B, T, _ = h.shape
    lam_init = 0.8 - 0.6 * math.exp(-0.3 * layer_idx)
    lp = P['diff_lambda'][j].astype(F32)
    lam = jnp.exp(jnp.sum(lp[0] * lp[1])) - jnp.exp(jnp.sum(lp[2] * lp[3])) + lam_init
    q = _rope((h @ P['diff_w_q'][j]).reshape(B, T, N_HEADS_B, 2, HEAD_QK), qpos)
    o = _attend(q, qpos, k_all, v_all, kpos, lam)
    o = o * lax.rsqrt(jnp.mean(o * o, axis=-1, keepdims=True) + NORM_EPS)
    o = o * P['diff_subln_g'][j].astype(F32) * (1.0 - lam_init)
    return o.reshape(B, T, V_W).astype(h.dtype) @ P['diff_w_o'][j]


def _trunk(x, c, past_len, wkv_init, shift_init, k_past, v_past, P):
    B, T, _ = x.shape
    dt = x.dtype
    pos = past_len + jnp.arange(T)
    sc = jax.nn.silu(c)
    new_wkv, new_shift = [], []
    k_new = v_new = k_all = v_all = kpos = None
    for l in range(DEPTH):
        if l == N_A:
            kv_mod = (sc @ P['kv_ada_w'] + P['kv_ada_b'])[:, None, :]
            kv_shift, kv_scale = jnp.split(kv_mod, 2, axis=-1)
            hk = _modulate(_rms(x, P['kv_norm_g']), kv_shift, kv_scale)
            kv = hk @ P['kv_w']
            k_new = _rope(kv[..., :QK_W].reshape(B, T, N_HEADS_B, 2, HEAD_QK), pos)
            v_new = kv[..., QK_W:].reshape(B, T, N_HEADS_B, HEAD_V)
            if k_past is None:
                k_all, v_all = k_new.astype(F32), v_new.astype(F32)
            else:
                k_all = jnp.concatenate([k_past.astype(F32), k_new.astype(F32)], axis=1)
                v_all = jnp.concatenate([v_past.astype(F32), v_new.astype(F32)], axis=1)
            kpos = jnp.arange(past_len + T)
        mod = (sc @ P['ada_w'][l] + P['ada_b'][l])[:, None, :]
        s1, c1, g1, s2, c2, g2, s3, c3, g3 = jnp.split(mod, 9, axis=-1)
        ng = P['norm_g'][l]
        hf = _modulate(_rms(x, ng[0]), s1, c1)
        x = x + 0.5 * g1 * _rms(_swiglu(hf, P['ffn_w_in'][l, 0], P['ffn_w_out'][l, 0]), ng[1])
        hm = _modulate(_rms(x, ng[2]), s2, c2)
        if l < N_A:
            out, S_T, last = _rwkv7_time_mix(hm, shift_init[l], wkv_init[l], l, P)
            new_wkv.append(S_T)
            new_shift.append(last)
        else:
            out = _diff_attn_layer(hm, pos, k_all, v_all, kpos, l - N_A, l, P)
        x = x + g2 * _rms(out, ng[3])
        hf = _modulate(_rms(x, ng[4]), s3, c3)
        x = x + 0.5 * g3 * _rms(_swiglu(hf, P['ffn_w_in'][l, 1], P['ffn_w_out'][l, 1]), ng[5])
    return x, jnp.stack(new_wkv), jnp.stack(new_shift), k_new, v_new


def setup_inputs(seed: int = 0) -> dict:
    key = jax.random.key(seed)
    ks = iter(jax.random.split(key, 64))
    D = D_MODEL

    def nrm(shape, scale):
        return jax.random.normal(next(ks), shape, F32) * scale

    def unif(shape, lo, hi):
        return jax.random.uniform(next(ks), shape, F32, lo, hi)

    return {
        "x_prompt": nrm((BATCH, SEQ, D), 1.0),
        "x_sample": nrm((DEC_BATCH, DEC_SEQ, D), 1.0),
        "c_prompt": nrm((BATCH, D), 1.0),
        "c_sample": nrm((DEC_BATCH, D), 1.0),
        "state_wkv": nrm((N_A, DEC_BATCH, RWKV_HEADS, RWKV_HEAD, RWKV_HEAD), 0.5),
        "state_shift": nrm((N_A, DEC_BATCH, 1, D), 1.0),
        "cache_k": nrm((DEC_BATCH, PAST_LEN, N_HEADS_B, 2, HEAD_QK), 1.0),
        "cache_v": nrm((DEC_BATCH, PAST_LEN, N_HEADS_B, HEAD_V), 1.0),
        "ada_w": nrm((DEPTH, D, 9 * D), 0.5 * D ** -0.5),
        "ada_b": nrm((DEPTH, 9 * D), 0.02),
        "norm_g": 1.0 + nrm((DEPTH, 6, D), 0.05),
        "ffn_w_in": nrm((DEPTH, 2, D, 2 * D_FF), D ** -0.5),
        "ffn_w_out": nrm((DEPTH, 2, D_FF, D), D_FF ** -0.5),
        "rwkv_mu": unif((N_A, 6, D), 0.0, 1.0),
        "rwkv_w_rkv": nrm((N_A, 3, D, D), D ** -0.5),
        "rwkv_w0": unif((N_A, D), -6.0, -1.0),
        "rwkv_w1": nrm((N_A, D, DECAY_LORA), D ** -0.5),
        "rwkv_w2": nrm((N_A, DECAY_LORA, D), 0.5 * DECAY_LORA ** -0.5),
        "rwkv_a0": nrm((N_A, D), 0.1),
        "rwkv_a1": nrm((N_A, D, A_LORA), D ** -0.5),
        "rwkv_a2": nrm((N_A, A_LORA, D), 0.5 * A_LORA ** -0.5),
        "rwkv_g1": nrm((N_A, D, GATE_LORA), D ** -0.5),
        "rwkv_g2": nrm((N_A, GATE_LORA, D), GATE_LORA ** -0.5),
        "rwkv_k_k": 0.85 + nrm((N_A, D), 0.05),
        "rwkv_k_a": 1.0 + nrm((N_A, D), 0.05),
        "rwkv_r_k": nrm((N_A, RWKV_HEADS, RWKV_HEAD), 0.1),
        "rwkv_ln_w": 1.0 + nrm((N_A, D), 0.05),
        "rwkv_ln_b": nrm((N_A, D), 0.02),
        "rwkv_w_o": nrm((N_A, D, D), D ** -0.5),
        "kv_ada_w": nrm((D, 2 * D), 0.5 * D ** -0.5),
        "kv_ada_b": nrm((2 * D,), 0.02),
        "kv_norm_g": 1.0 + nrm((D,), 0.05),
        "kv_w": nrm((D, QK_W + V_W), D ** -0.5),
        "diff_w_q": nrm((N_B, D, QK_W), D ** -0.5),
        "diff_lambda": nrm((N_B, 4, HEAD_QK), 0.1),
        "diff_subln_g": 1.0 + nrm((N_B, HEAD_V), 0.05),
        "diff_w_o": nrm((N_B, V_W, D), V_W ** -0.5),
    }


def reference(x_prompt, x_sample, c_prompt, c_sample, state_wkv, state_shift, cache_k, cache_v,
              ada_w, ada_b, norm_g, ffn_w_in, ffn_w_out,
              rwkv_mu, rwkv_w_rkv, rwkv_w0, rwkv_w1, rwkv_w2, rwkv_a0, rwkv_a1, rwkv_a2,
              rwkv_g1, rwkv_g2, rwkv_k_k, rwkv_k_a, rwkv_r_k, rwkv_ln_w, rwkv_ln_b, rwkv_w_o,
              kv_ada_w, kv_ada_b, kv_norm_g, kv_w,
              diff_w_q, diff_lambda, diff_subln_g, diff_w_o):
    P = dict(ada_w=ada_w, ada_b=ada_b, norm_g=norm_g, ffn_w_in=ffn_w_in, ffn_w_out=ffn_w_out,
             rwkv_mu=rwkv_mu, rwkv_w_rkv=rwkv_w_rkv, rwkv_w0=rwkv_w0, rwkv_w1=rwkv_w1,
             rwkv_w2=rwkv_w2, rwkv_a0=rwkv_a0, rwkv_a1=rwkv_a1, rwkv_a2=rwkv_a2,
             rwkv_g1=rwkv_g1, rwkv_g2=rwkv_g2, rwkv_k_k=rwkv_k_k, rwkv_k_a=rwkv_k_a,
             rwkv_r_k=rwkv_r_k, rwkv_ln_w=rwkv_ln_w, rwkv_ln_b=rwkv_ln_b, rwkv_w_o=rwkv_w_o,
             kv_ada_w=kv_ada_w, kv_ada_b=kv_ada_b, kv_norm_g=kv_norm_g, kv_w=kv_w,
             diff_w_q=diff_w_q, diff_lambda=diff_lambda, diff_subln_g=diff_subln_g,
             diff_w_o=diff_w_o)
    Bp = x_prompt.shape[0]
    wkv0 = jnp.zeros((N_A, Bp, RWKV_HEADS, RWKV_HEAD, RWKV_HEAD), F32)
    shift0 = jnp.zeros((N_A, Bp, 1, D_MODEL), x_prompt.dtype)
    y_prompt, wkv_prompt, shift_prompt, k_prompt, v_prompt = _trunk(
        x_prompt, c_prompt, 0, wkv0, shift0, None, None, P)
    y_sample, wkv_sample, shift_sample, k_sample, v_sample = _trunk(
        x_sample, c_sample, PAST_LEN, state_wkv, state_shift, cache_k, cache_v, P)
    return (y_prompt, y_sample, wkv_prompt, shift_prompt, k_prompt, v_prompt,
            wkv_sample, shift_sample, k_sample, v_sample)
```

```python
import functools
import math

import jax
import jax.numpy as jnp
from jax import lax
from jax.experimental import pallas as pl
from jax.experimental.pallas import tpu as pltpu

F32 = jnp.float32
BF16 = jnp.bfloat16
HI = lax.Precision.HIGHEST

CHUNK = 64
RWKV_HEAD = 64
GN_EPS = 64e-5
N_HEADS_B = 8
HEAD_QK = 64
HEAD_V = 128
ROT_DIM = HEAD_QK // 4
ROPE_THETA = 500000.0
ATTN_SCALE = HEAD_QK ** -0.5
NORM_EPS = 1e-6
NEG_INF = -1e30

LANES = 128
VMEM_LIMIT = 56 * 1024 * 1024


def _cparams(sem):
    return pltpu.CompilerParams(dimension_semantics=sem, vmem_limit_bytes=VMEM_LIMIT)


def _tile(B, T, rows):
    if T >= rows:
        assert T % rows == 0
        return 1, rows
    bb = min(B, max(1, rows // T))
    while B % bb:
        bb -= 1
    return bb, T


def _rms(x, g):
    return x * lax.rsqrt(jnp.mean(x * x, axis=-1, keepdims=True) + NORM_EPS) * g


def _sigmoid(x):
    return 1.0 / (1.0 + jnp.exp(-x))


def _dot(a, b):
    return jnp.dot(a, b, preferred_element_type=F32)


def _dot_hi(a, b):
    return jnp.dot(a, b, preferred_element_type=F32, precision=HI)


def _dot_nt_hi(a, b):
    return lax.dot_general(a, b, (((1,), (1,)), ((), ())), preferred_element_type=F32, precision=HI)


def _dot_tn_hi(a, b):
    return lax.dot_general(a, b, (((0,), (0,)), ((), ())), preferred_element_type=F32, precision=HI)


def _ada_kernel(c_ref, w_ref, b_ref, o_ref):
    c = c_ref[...]
    sc = (c * _sigmoid(c)).astype(BF16)
    o_ref[...] = _dot(sc, w_ref[...].astype(BF16)) + b_ref[...]


def _ada(c, w, b, tn=1024):
    L, D, N = w.shape
    M = c.shape[0]
    return pl.pallas_call(
        _ada_kernel,
        grid=(L, N // tn),
        in_specs=[pl.BlockSpec((M, D), lambda l, j: (0, 0)),
                  pl.BlockSpec((None, D, tn), lambda l, j: (l, 0, j)),
                  pl.BlockSpec((None, 1, tn), lambda l, j: (l, 0, j))],
        out_specs=pl.BlockSpec((None, M, tn), lambda l, j: (l, 0, j)),
        out_shape=jax.ShapeDtypeStruct((L, M, N), F32),
        compiler_params=_cparams(("parallel", "parallel")),
        name="ada",
    )(c, w, b)


def _ffn_kernel(x_ref, s_ref, c_ref, g_ref, ngpre_ref, ngpost_ref, wg_ref, wu_ref, wo_ref,
                o_ref, h_scr, acc_scr):
    j = pl.program_id(2)
    bb, tt, D = x_ref.shape

    @pl.when(j == 0)
    def _():
        h = _rms(x_ref[...], ngpre_ref[...]) * (1.0 + c_ref[...]) + s_ref[...]
        h_scr[...] = h.reshape(bb * tt, D).astype(BF16)
        acc_scr[...] = jnp.zeros_like(acc_scr)

    h = h_scr[...]
    gate = _dot(h, wg_ref[...])
    up = _dot(h, wu_ref[...])
    act = (gate * _sigmoid(gate)) * up
    acc_scr[...] += _dot(act.astype(BF16), wo_ref[...])

    @pl.when(j == pl.num_programs(2) - 1)
    def _():
        f = acc_scr[...].reshape(bb, tt, D)
        o_ref[...] = x_ref[...] + 0.5 * g_ref[...] * _rms(f, ngpost_ref[...])


def _ffn(x, s, c, g, ng_pre, ng_post, w_in, w_out, rows=512, tf=1408):
    B, T, D = x.shape
    Fh = w_out.shape[0]
    nf = Fh // tf
    bb, tt = _tile(B, T, rows)
    xspec = pl.BlockSpec((bb, tt, D), lambda b, i, j: (b, i, 0))
    mspec = pl.BlockSpec((bb, 1, D), lambda b, i, j: (b, 0, 0))
    gspec = pl.BlockSpec((1, D), lambda b, i, j: (0, 0))
    return pl.pallas_call(
        _ffn_kernel,
        grid=(B // bb, T // tt, nf),
        in_specs=[xspec, mspec, mspec, mspec, gspec, gspec,
                  pl.BlockSpec((D, tf), lambda b, i, j: (0, j)),
                  pl.BlockSpec((D, tf), lambda b, i, j: (0, nf + j)),
                  pl.BlockSpec((tf, D), lambda b, i, j: (j, 0))],
        out_specs=xspec,
        out_shape=jax.ShapeDtypeStruct((B, T, D), F32),
        scratch_shapes=[pltpu.VMEM((bb * tt, D), BF16), pltpu.VMEM((bb * tt, D), F32)],
        compiler_params=_cparams(("parallel", "parallel", "arbitrary")),
        name="ffn",
    )(x, s, c, g, ng_pre, ng_post, w_in, w_in, w_out)


def _rwkv_pre_kernel(x_ref, s_ref, c_ref, ng_ref, shift_ref, mu_ref, wrkv_ref, w0_ref, w1_ref, w2_ref,
                     a0_ref, a1_ref, a2_ref, g1_ref, g2_ref,
                     r_ref, k_ref, v_ref, a_ref, lw_ref, g_ref, shift_out_ref, prev_scr):
    bb, tt, D = x_ref.shape

    @pl.when(pl.program_id(1) == 0)
    def _():
        prev_scr[...] = shift_ref[...]

    hm = _rms(x_ref[...], ng_ref[...]) * (1.0 + c_ref[...]) + s_ref[...]
    row = lax.broadcasted_iota(jnp.int32, hm.shape, 1)
    xprev = jnp.where(row == 0, prev_scr[...], pltpu.roll(hm, 1, axis=1))
    last = hm[:, tt - 1:tt, :]
    prev_scr[...] = last
    shift_out_ref[...] = last
    dx = xprev - hm

    def mix(j):
        return (hm + dx * mu_ref[j:j + 1, :]).reshape(bb * tt, D).astype(BF16)

    def out(ref, val):
        ref[...] = val.reshape(bb, tt, D)

    out(r_ref, _dot(mix(0), wrkv_ref[0]))
    out(k_ref, _dot(mix(2), wrkv_ref[1]))
    out(v_ref, _dot(mix(3), wrkv_ref[2]))
    z = w0_ref[...] + _dot(jnp.tanh(_dot(mix(1), w1_ref[...])).astype(BF16), w2_ref[...])
    nz = -z
    softplus = jnp.maximum(nz, 0.0) + jnp.log1p(jnp.exp(-jnp.abs(nz)))
    out(lw_ref, -jnp.exp(-softplus - 0.5))
    out(a_ref, _sigmoid(a0_ref[...] + _dot(_dot(mix(4), a1_ref[...]).astype(BF16), a2_ref[...])))
    out(g_ref, _dot(_sigmoid(_dot(mix(5), g1_ref[...])).astype(BF16), g2_ref[...]))


def _rwkv_pre(x, s, c, ng, shift_in, mu, wrkv, w0, w1, w2, a0, a1, a2, g1, g2, rows=256):
    B, T, D = x.shape
    bb, tt = _tile(B, T, rows)
    xspec = pl.BlockSpec((bb, tt, D), lambda b, i: (b, i, 0))
    mspec = pl.BlockSpec((bb, 1, D), lambda b, i: (b, 0, 0))

    def full(a):
        return pl.BlockSpec(a.shape, lambda b, i: (0,) * a.ndim)

    big = jax.ShapeDtypeStruct((B, T, D), F32)
    return pl.pallas_call(
        _rwkv_pre_kernel,
        grid=(B // bb, T // tt),
        in_specs=[xspec, mspec, mspec, full(ng), mspec, full(mu), full(wrkv), full(w0), full(w1), full(w2),
                  full(a0), full(a1), full(a2), full(g1), full(g2)],
        out_specs=[xspec] * 6 + [mspec],
        out_shape=[big] * 6 + [jax.ShapeDtypeStruct((B, 1, D), F32)],
        scratch_shapes=[pltpu.VMEM((bb, 1, D), F32)],
        compiler_params=_cparams(("parallel", "arbitrary")),
        name="rwkv_pre",
    )(x, s, c, ng, shift_in, mu, wrkv, w0, w1, w2, a0, a1, a2, g1, g2)


def _wkv_kernel(r_ref, k_ref, v_ref, a_ref, lw_ref, g_ref, s0_ref, kk_ref, ka_ref, rk_ref, lnw_ref, lnb_ref,
                o_ref, sout_ref, s_scr):
    C = CHUNK
    npairs = s_scr.shape[0]

    @pl.when(pl.program_id(1) == 0)
    def _():
        s_scr[...] = s0_ref[...]

    ri = lax.broadcasted_iota(jnp.int32, (LANES, LANES), 0)
    ci = lax.broadcasted_iota(jnp.int32, (LANES, LANES), 1)
    same_head = (ri // C) == (ci // C)
    head_ones = same_head.astype(F32)
    head_mean = head_ones * (1.0 / RWKV_HEAD)
    tril_incl = same_head & ((ci % C) <= (ri % C))
    tril_strict = same_head & ((ci % C) < (ri % C))
    eye = (ri == ci).astype(F32)
    r64 = lax.broadcasted_iota(jnp.int32, (C, C), 0)
    c64 = lax.broadcasted_iota(jnp.int32, (C, C), 1)
    cumsum_mat = (c64 <= r64).astype(F32)
    lane = lax.broadcasted_iota(jnp.int32, (C, LANES), 1)
    first_head = lane < RWKV_HEAD

    def stack(x):
        return jnp.concatenate([jnp.where(first_head, x, 0.0), jnp.where(first_head, 0.0, x)], axis=0)

    def level_mask(b):
        return ((ri // (2 * b)) == (ci // (2 * b))) & ((ri % (2 * b)) >= b) & ((ci % (2 * b)) < b)

    for p in range(npairs):
        sl = slice(p * LANES, (p + 1) * LANES)
        r, k, v, a, lw, g = (ref[:, sl] for ref in (r_ref, k_ref, v_ref, a_ref, lw_ref, g_ref))
        kk = k * kk_ref[:, sl]
        norm = jnp.sqrt(_dot_hi(kk * kk, head_ones))
        kk = kk / jnp.maximum(norm, 1e-12)
        kmod = k * (1.0 + (a - 1.0) * ka_ref[:, sl])
        bvec = kk * a

        cum = _dot_hi(cumsum_mat, lw)
        total = cum[C - 1:C, :]
        inv = jnp.exp(-cum)
        rem = jnp.exp(total - cum)
        a2 = stack(-kk * jnp.exp(cum - lw))
        b2 = stack(bvec * inv)
        k2 = stack(kmod * inv)
        r2 = stack(r * jnp.exp(cum))
        v2 = stack(v)
        bh2 = stack(bvec * rem)
        kh2 = stack(kmod * rem)

        lab = jnp.where(tril_strict, _dot_nt_hi(a2, b2), 0.0)
        lak = jnp.where(tril_strict, _dot_nt_hi(a2, k2), 0.0)
        mrb = jnp.where(tril_incl, _dot_nt_hi(r2, b2), 0.0)
        mrk = jnp.where(tril_incl, _dot_nt_hi(r2, k2), 0.0)

        x = eye + jnp.where(level_mask(1), lab, 0.0)
        for b in (2, 4, 8, 16, 32):
            x = x + _dot_hi(_dot_hi(x, jnp.where(level_mask(b), lab, 0.0)), x)

        s = s_scr[p]
        u2 = _dot_hi(x, _dot_nt_hi(a2, s) + _dot_hi(lak, v2))
        y2 = _dot_nt_hi(r2, s) + _dot_hi(mrb, u2) + _dot_hi(mrk, v2)
        y = y2[:C] + y2[C:]
        s_scr[p] = s * jnp.exp(total) + _dot_tn_hi(u2, bh2) + _dot_tn_hi(v2, kh2)

        mean = _dot_hi(y, head_mean)
        d = y - mean
        var = _dot_hi(d * d, head_mean)
        yn = d * lax.rsqrt(var + GN_EPS) * lnw_ref[:, sl] + lnb_ref[:, sl]
        bonus = _dot_hi(r * kmod * rk_ref[:, sl], head_ones) * v
        o_ref[:, sl] = (yn + bonus) * g

    @pl.when(pl.program_id(1) == pl.num_programs(1) - 1)
    def _():
        sout_ref[...] = s_scr[...]


def _wkv(r, k, v, a, lw, g, s0_bd, k_k, k_a, r_k, ln_w, ln_b):
    B, T, D = r.shape
    npairs = D // LANES
    xspec = pl.BlockSpec((None, CHUNK, D), lambda b, i: (b, i, 0))
    sspec = pl.BlockSpec((None, npairs, LANES, LANES), lambda b, i: (b, 0, 0, 0))
    pspec = pl.BlockSpec((1, D), lambda b, i: (0, 0))
    return pl.pallas_call(
        _wkv_kernel,
        grid=(B, T // CHUNK),
        in_specs=[xspec] * 6 + [sspec] + [pspec] * 5,
        out_specs=[xspec, sspec],
        out_shape=[jax.ShapeDtypeStruct((B, T, D), F32),
                   jax.ShapeDtypeStruct((B, npairs, LANES, LANES), F32)],
        scratch_shapes=[pltpu.VMEM((npairs, LANES, LANES), F32)],
        compiler_params=_cparams(("parallel", "arbitrary")),
        name="wkv",
    )(r, k, v, a, lw, g, s0_bd, k_k, k_a, r_k, ln_w, ln_b)


def _state_to_blockdiag(s):
    B, H, N, _ = s.shape
    s = s.reshape(B, H // 2, 2, N, N)
    z = jnp.zeros_like(s[:, :, 0])
    top = jnp.concatenate([s[:, :, 0], z], axis=-1)
    bot = jnp.concatenate([z, s[:, :, 1]], axis=-1)
    return jnp.concatenate([top, bot], axis=-2)


def _state_from_blockdiag(s):
    B, P, N2, _ = s.shape
    N = N2 // 2
    return jnp.stack([s[:, :, :N, :N], s[:, :, N:, N:]], axis=2).reshape(B, 2 * P, N, N)


def _proj_res_kernel(x_ref, a_ref, g_ref, ng_ref, w_ref, o_ref):
    bb, tt, D = x_ref.shape
    a = a_ref[...].reshape(bb * tt, a_ref.shape[-1]).astype(BF16)
    out = _dot(a, w_ref[...]).reshape(bb, tt, D)
    o_ref[...] = x_ref[...] + g_ref[...] * _rms(out, ng_ref[...])


def _proj_res(x, a, g, ng, w, rows=512):
    B, T, D = x.shape
    bb, tt = _tile(B, T, rows)
    xspec = pl.BlockSpec((bb, tt, D), lambda b, i: (b, i, 0))
    return pl.pallas_call(
        _proj_res_kernel,
        grid=(B // bb, T // tt),
        in_specs=[xspec, pl.BlockSpec((bb, tt, a.shape[-1]), lambda b, i: (b, i, 0)),
                  pl.BlockSpec((bb, 1, D), lambda b, i: (b, 0, 0)),
                  pl.BlockSpec((1, D), lambda b, i: (0, 0)),
                  pl.BlockSpec(w.shape, lambda b, i: (0, 0))],
        out_specs=xspec,
        out_shape=jax.ShapeDtypeStruct((B, T, D), F32),
        compiler_params=_cparams(("parallel", "parallel")),
        name="proj_res",
    )(x, a, g, ng, w)


def _rope_tables(pos):
    half = ROT_DIM // 2
    inv = ROPE_THETA ** (-jnp.arange(half, dtype=F32) * 2.0 / ROT_DIM)
    ang = pos.astype(F32)[:, None] * inv[None, :]
    cos, sin = jnp.cos(ang), jnp.sin(ang)
    T = pos.shape[0]
    ones = jnp.ones((T, HEAD_QK - ROT_DIM), F32)
    zeros = jnp.zeros((T, HEAD_QK - ROT_DIM), F32)
    zh = jnp.zeros((T, half), F32)
    c = jnp.concatenate([cos, cos, ones], axis=-1)
    s1 = jnp.concatenate([-sin, zh, zeros], axis=-1)
    s2 = jnp.concatenate([zh, sin, zeros], axis=-1)
    return tuple(jnp.concatenate([t, t], axis=-1) for t in (c, s1, s2))


def _rope_rows(y, c, s1, s2):
    half = ROT_DIM // 2
    outs = []
    for p in range(y.shape[-1] // LANES):
        xs = y[:, p * LANES:(p + 1) * LANES]
        outs.append(xs * c + pltpu.roll(xs, LANES - half, axis=1) * s1 + pltpu.roll(xs, half, axis=1) * s2)
    return jnp.concatenate(outs, axis=-1)


def _nmm_kv_kernel(x_ref, s_ref, c_ref, ng_ref, w_ref, tc_ref, ts1_ref, ts2_ref,
                   k_ref, v_ref, kb_ref, vb_ref):
    bb, tt, D = x_ref.shape
    h = _rms(x_ref[...], ng_ref[...]) * (1.0 + c_ref[...]) + s_ref[...]
    kv = _dot(h.reshape(bb * tt, D).astype(BF16), w_ref[...])
    nk = k_ref.shape[-1]
    k = _rope_rows(kv[:, :nk], tc_ref[...], ts1_ref[...], ts2_ref[...]).reshape(bb, tt, nk)
    v = kv[:, nk:].reshape(bb, tt, v_ref.shape[-1])
    k_ref[...] = k
    v_ref[...] = v
    kb_ref[...] = k.astype(BF16)
    vb_ref[...] = v.astype(BF16)


def _nmm_q_kernel(x_ref, s_ref, c_ref, ng_ref, w_ref, tc_ref, ts1_ref, ts2_ref, q_ref):
    bb, tt, D = x_ref.shape
    h = _rms(x_ref[...], ng_ref[...]) * (1.0 + c_ref[...]) + s_ref[...]
    q = _dot(h.reshape(bb * tt, D).astype(BF16), w_ref[...])
    q = _rope_rows(q, tc_ref[...], ts1_ref[...], ts2_ref[...]) * ATTN_SCALE
    q_ref[...] = q.reshape(bb, tt, q_ref.shape[-1]).astype(BF16)


def _nmm_rope(body, x, s, c, ng, w, pos, out_widths, out_dtypes, rows=512):
    B, T, D = x.shape
    bb, tt = _tile(B, T, rows)
    tabs = _rope_tables(pos)
    if bb > 1:
        tabs = tuple(jnp.tile(t, (bb, 1)) for t in tabs)
        tspec = pl.BlockSpec((bb * tt, LANES), lambda b, i: (0, 0))
    else:
        tspec = pl.BlockSpec((tt, LANES), lambda b, i: (i, 0))
    xspec = pl.BlockSpec((bb, tt, D), lambda b, i: (b, i, 0))
    mspec = pl.BlockSpec((bb, 1, D), lambda b, i: (b, 0, 0))
    return pl.pallas_call(
        body,
        grid=(B // bb, T // tt),
        in_specs=[xspec, mspec, mspec, pl.BlockSpec((1, D), lambda b, i: (0, 0)),
                  pl.BlockSpec(w.shape, lambda b, i: (0, 0)), tspec, tspec, tspec],
        out_specs=[pl.BlockSpec((bb, tt, n), lambda b, i: (b, i, 0)) for n in out_widths],
        out_shape=[jax.ShapeDtypeStruct((B, T, n), dt) for n, dt in zip(out_widths, out_dtypes)],
        compiler_params=_cparams(("parallel", "parallel")),
        name="nmm_rope",
    )(x, s, c, ng, w, *tabs)


def _attn_kernel(q_ref, k_ref, v_ref, lam_ref, sg_ref, o_ref, m_scr, l_scr, acc_scr, *, q_off, tk, lam_init):
    tq = q_ref.shape[0]
    qi = pl.program_id(2)
    q0 = q_off + qi * tq
    lane = lax.broadcasted_iota(jnp.int32, (tq, LANES), 1)
    q = q_ref[...]
    zero = jnp.zeros_like(q)
    qs = jnp.concatenate([jnp.where(lane < HEAD_QK, q, zero), jnp.where(lane < HEAD_QK, zero, q)], axis=0)

    m_scr[...] = jnp.full_like(m_scr, NEG_INF)
    l_scr[...] = jnp.zeros_like(l_scr)
    acc_scr[...] = jnp.zeros_like(acc_scr)

    def update(kt, vt, mask):
        s = lax.dot_general(qs, kt, (((1,), (1,)), ((), ())), preferred_element_type=F32)
        if mask is not None:
            s = jnp.where(mask, s, NEG_INF)
        m_old = m_scr[...]
        m_new = jnp.maximum(m_old, jnp.max(s, axis=-1, keepdims=True))
        p = jnp.exp(s - m_new)
        alpha = jnp.exp(m_old - m_new)
        l_scr[...] = alpha * l_scr[...] + jnp.sum(p, axis=-1, keepdims=True)
        acc_scr[...] = alpha * acc_scr[...] + _dot(p.astype(BF16), vt)
        m_scr[...] = m_new

    def full_tile(j, carry):
        start = pl.multiple_of(j * tk, tk)
        update(k_ref[pl.ds(start, tk), :], v_ref[pl.ds(start, tk), :], None)
        return carry

    lax.fori_loop(0, q0 // tk, full_tile, 0)

    start = pl.multiple_of(q0, CHUNK)
    row = lax.broadcasted_iota(jnp.int32, (2 * tq, tq), 0)
    col = lax.broadcasted_iota(jnp.int32, (2 * tq, tq), 1)
    visible = (col // CHUNK) <= ((row % tq) // CHUNK)
    update(k_ref[pl.ds(start, tq), :], v_ref[pl.ds(start, tq), :], visible)

    lp = lam_ref[...]
    lam = (jnp.exp(jnp.sum(lp[0:1] * lp[1:2], axis=-1, keepdims=True))
           - jnp.exp(jnp.sum(lp[2:3] * lp[3:4], axis=-1, keepdims=True)) + lam_init)
    o = acc_scr[...] / l_scr[...]
    o = o[:tq] - lam * o[tq:]
    o = o * lax.rsqrt(jnp.mean(o * o, axis=-1, keepdims=True) + NORM_EPS)
    o_ref[...] = o * sg_ref[...] * (1.0 - lam_init)


def _attn(q, kb, vb, lam_p, subln_g, q_off, lam_init, tile=512):
    B, T, W = q.shape
    Tk = kb.shape[1]
    H = W // LANES
    tq = min(T, tile)
    tk = tile
    assert T % tq == 0 and q_off % tk == 0 and (tq == tk or T == tq) and Tk == q_off + T
    qspec = pl.BlockSpec((None, tq, LANES), lambda b, h, i: (b, i, h))
    kspec = pl.BlockSpec((None, Tk, LANES), lambda b, h, i: (b, 0, h))
    return pl.pallas_call(
        functools.partial(_attn_kernel, q_off=q_off, tk=tk, lam_init=lam_init),
        grid=(B, H, T // tq),
        in_specs=[qspec, kspec, kspec,
                  pl.BlockSpec(lam_p.shape, lambda b, h, i: (0, 0)),
                  pl.BlockSpec((1, LANES), lambda b, h, i: (0, 0))],
        out_specs=qspec,
        out_shape=jax.ShapeDtypeStruct((B, T, W), F32),
        scratch_shapes=[pltpu.VMEM((2 * tq, 1), F32), pltpu.VMEM((2 * tq, 1), F32),
                        pltpu.VMEM((2 * tq, LANES), F32)],
        compiler_params=_cparams(("parallel", "parallel", "arbitrary")),
        name="attn",
    )(q, kb, vb, lam_p, subln_g)


def _trunk(x, mods, kv_mod, past_len, wkv0, shift0, k_past, v_past, P):
    B, T, D = x.shape
    pos = past_len + jnp.arange(T)

    def chunks(m, n):
        return [m[:, None, j * D:(j + 1) * D] for j in range(n)]

    row = lambda a: a.reshape(1, -1)
    for l in range(2):
        s1, c1, g1, s2, c2, g2, s3, c3, g3 = chunks(mods[l], 9)
        ng = P['norm_g'][l]
        if l == 1:
            kv_shift, kv_scale = chunks(kv_mod, 2)
            k_new, v_new, kb, vb = _nmm_rope(_nmm_kv_kernel, x, kv_shift, kv_scale, row(P['kv_norm_g']),
                                             P['kv_w'], pos, (D, D, D, D), (F32, F32, BF16, BF16))
            if k_past is not None:
                kb = jnp.concatenate([k_past.reshape(B, past_len, D).astype(BF16), kb], axis=1)
                vb = jnp.concatenate([v_past.reshape(B, past_len, D).astype(BF16), vb], axis=1)
        x = _ffn(x, s1, c1, g1, row(ng[0]), row(ng[1]), P['ffn_w_in'][l][0], P['ffn_w_out'][l][0])
        if l == 0:
            r, k, v, a, lw, g, shift = _rwkv_pre(
                x, s2, c2, row(ng[2]), shift0, P['rwkv_mu'], P['rwkv_w_rkv'], row(P['rwkv_w0']),
                P['rwkv_w1'], P['rwkv_w2'], row(P['rwkv_a0']), P['rwkv_a1'], P['rwkv_a2'],
                P['rwkv_g1'], P['rwkv_g2'])
            o, s_bd = _wkv(r, k, v, a, lw, g, _state_to_blockdiag(wkv0), row(P['rwkv_k_k']),
                           row(P['rwkv_k_a']), row(P['rwkv_r_k']), row(P['rwkv_ln_w']), row(P['rwkv_ln_b']))
            wkv = _state_from_blockdiag(s_bd)
            x = _proj_res(x, o, g2, row(ng[3]), P['rwkv_w_o'])
        else:
            (q,) = _nmm_rope(_nmm_q_kernel, x, s2, c2, row(ng[2]), P['diff_w_q'], pos, (D,), (BF16,))
            lam_init = 0.8 - 0.6 * math.exp(-0.3 * l)
            o = _attn(q, kb, vb, P['diff_lambda'], row(P['diff_subln_g']), past_len, lam_init)
            x = _proj_res(x, o, g2, row(ng[3]), P['diff_w_o'])
        x = _ffn(x, s3, c3, g3, row(ng[4]), row(ng[5]), P['ffn_w_in'][l][1], P['ffn_w_out'][l][1])
    return (x, wkv[None], shift[None],
            k_new.reshape(B, T, N_HEADS_B, 2, HEAD_QK), v_new.reshape(B, T, N_HEADS_B, HEAD_V))


def kernel(x_prompt, x_sample, c_prompt, c_sample, state_wkv, state_shift, cache_k, cache_v, ada_w, ada_b, norm_g, ffn_w_in, ffn_w_out, rwkv_mu, rwkv_w_rkv, rwkv_w0, rwkv_w1, rwkv_w2, rwkv_a0, rwkv_a1, rwkv_a2, rwkv_g1, rwkv_g2, rwkv_k_k, rwkv_k_a, rwkv_r_k, rwkv_ln_w, rwkv_ln_b, rwkv_w_o, kv_ada_w, kv_ada_b, kv_norm_g, kv_w, diff_w_q, diff_lambda, diff_subln_g, diff_w_o):
    Bp, _, D = x_prompt.shape
    Bs = x_sample.shape[0]
    H = D // RWKV_HEAD
    bf = lambda w: w.astype(BF16)
    P = dict(norm_g=norm_g, ffn_w_in=bf(ffn_w_in), ffn_w_out=bf(ffn_w_out),
             rwkv_mu=rwkv_mu[0], rwkv_w_rkv=bf(rwkv_w_rkv[0]), rwkv_w0=rwkv_w0[0], rwkv_w1=bf(rwkv_w1[0]),
             rwkv_w2=bf(rwkv_w2[0]), rwkv_a0=rwkv_a0[0], rwkv_a1=bf(rwkv_a1[0]), rwkv_a2=bf(rwkv_a2[0]),
             rwkv_g1=bf(rwkv_g1[0]), rwkv_g2=bf(rwkv_g2[0]), rwkv_k_k=rwkv_k_k[0], rwkv_k_a=rwkv_k_a[0],
             rwkv_r_k=rwkv_r_k[0], rwkv_ln_w=rwkv_ln_w[0], rwkv_ln_b=rwkv_ln_b[0], rwkv_w_o=bf(rwkv_w_o[0]),
             kv_norm_g=kv_norm_g, kv_w=bf(kv_w), diff_w_q=bf(diff_w_q[0]), diff_lambda=diff_lambda[0],
             diff_subln_g=diff_subln_g[0], diff_w_o=bf(diff_w_o[0]))

    c_all = jnp.concatenate([c_prompt, c_sample], axis=0)
    mods = _ada(c_all, ada_w, ada_b[:, None, :])
    kv_mod = _ada(c_all, kv_ada_w[None], kv_ada_b[None, None, :])[0]

    wkv0 = jnp.zeros((Bp, H, RWKV_HEAD, RWKV_HEAD), F32)
    shift0 = jnp.zeros((Bp, 1, D), F32)
    y_p, wkv_p, shift_p, k_p, v_p = _trunk(
        x_prompt, [mods[0, :Bp], mods[1, :Bp]], kv_mod[:Bp], 0, wkv0, shift0, None, None, P)
    y_s, wkv_s, shift_s, k_s, v_s = _trunk(
        x_sample, [mods[0, Bp:], mods[1, Bp:]], kv_mod[Bp:], cache_k.shape[1],
        state_wkv[0], state_shift[0], cache_k, cache_v, P)
    return (y_p, y_s, wkv_p, shift_p, k_p, v_p, wkv_s, shift_s, k_s, v_s)
```

```python
import functools
import math

import jax
import jax.numpy as jnp
from jax import lax
from jax.experimental import pallas as pl
from jax.experimental.pallas import tpu as pltpu

F32 = jnp.float32
BF16 = jnp.bfloat16
HI = lax.Precision.HIGHEST

CHUNK = 64
RWKV_HEAD = 64
GN_EPS = 64e-5
N_HEADS_B = 8
HEAD_QK = 64
HEAD_V = 128
ROT_DIM = HEAD_QK // 4
ROPE_THETA = 500000.0
ATTN_SCALE = HEAD_QK ** -0.5
LOG2E = math.log2(math.e)
NORM_EPS = 1e-6
NEG_INF = -1e30

LANES = 128
VMEM_LIMIT = 56 * 1024 * 1024


def _cparams(sem):
    return pltpu.CompilerParams(dimension_semantics=sem, vmem_limit_bytes=VMEM_LIMIT)


def _tile(B, T, rows):
    if T >= rows:
        assert T % rows == 0
        return 1, rows
    bb = min(B, max(1, rows // T))
    while B % bb:
        bb -= 1
    return bb, T


def _rms(x, g):
    return x * lax.rsqrt(jnp.mean(x * x, axis=-1, keepdims=True) + NORM_EPS) * g


def _sigmoid(x):
    return 1.0 / (1.0 + jnp.exp(-x))


def _dot(a, b):
    return jnp.dot(a, b, preferred_element_type=F32)


def _dot_hi(a, b):
    return jnp.dot(a, b, preferred_element_type=F32, precision=HI)


def _dot_nt_hi(a, b):
    return lax.dot_general(a, b, (((1,), (1,)), ((), ())), preferred_element_type=F32, precision=HI)


def _dot_tn_hi(a, b):
    return lax.dot_general(a, b, (((0,), (0,)), ((), ())), preferred_element_type=F32, precision=HI)


def _ada_kernel(c_ref, w_ref, b_ref, o_ref):
    c = c_ref[...]
    sc = (c * _sigmoid(c)).astype(BF16)
    o_ref[...] = _dot(sc, w_ref[...].astype(BF16)) + b_ref[...]


def _ada(c, w, b, tn=1024):
    L, D, N = w.shape
    M = c.shape[0]
    return pl.pallas_call(
        _ada_kernel,
        grid=(L, N // tn),
        in_specs=[pl.BlockSpec((M, D), lambda l, j: (0, 0)),
                  pl.BlockSpec((None, D, tn), lambda l, j: (l, 0, j)),
                  pl.BlockSpec((None, 1, tn), lambda l, j: (l, 0, j))],
        out_specs=pl.BlockSpec((None, M, tn), lambda l, j: (l, 0, j)),
        out_shape=jax.ShapeDtypeStruct((L, M, N), F32),
        compiler_params=_cparams(("parallel", "parallel")),
        name="ada",
    )(c, w, b)


def _ffn_kernel(x_ref, s_ref, c_ref, g_ref, ngpre_ref, ngpost_ref, wg_ref, wu_ref, wo_ref,
                o_ref, h_scr, acc_scr):
    j = pl.program_id(2)
    bb, tt, D = x_ref.shape

    @pl.when(j == 0)
    def _():
        h = _rms(x_ref[...], ngpre_ref[...]) * (1.0 + c_ref[...]) + s_ref[...]
        h_scr[...] = h.reshape(bb * tt, D).astype(BF16)
        acc_scr[...] = jnp.zeros_like(acc_scr)

    h = h_scr[...]
    gate = _dot(h, wg_ref[...])
    up = _dot(h, wu_ref[...])
    act = (gate * _sigmoid(gate)) * up
    acc_scr[...] += _dot(act.astype(BF16), wo_ref[...])

    @pl.when(j == pl.num_programs(2) - 1)
    def _():
        f = acc_scr[...].reshape(bb, tt, D)
        o_ref[...] = x_ref[...] + 0.5 * g_ref[...] * _rms(f, ngpost_ref[...])


def _ffn(x, s, c, g, ng_pre, ng_post, w_in, w_out, rows=512, tf=1408):
    B, T, D = x.shape
    Fh = w_out.shape[0]
    nf = Fh // tf
    bb, tt = _tile(B, T, rows)
    xspec = pl.BlockSpec((bb, tt, D), lambda b, i, j: (b, i, 0))
    mspec = pl.BlockSpec((bb, 1, D), lambda b, i, j: (b, 0, 0))
    gspec = pl.BlockSpec((1, D), lambda b, i, j: (0, 0))
    return pl.pallas_call(
        _ffn_kernel,
        grid=(B // bb, T // tt, nf),
        in_specs=[xspec, mspec, mspec, mspec, gspec, gspec,
                  pl.BlockSpec((D, tf), lambda b, i, j: (0, j)),
                  pl.BlockSpec((D, tf), lambda b, i, j: (0, nf + j)),
                  pl.BlockSpec((tf, D), lambda b, i, j: (j, 0))],
        out_specs=xspec,
        out_shape=jax.ShapeDtypeStruct((B, T, D), F32),
        scratch_shapes=[pltpu.VMEM((bb * tt, D), BF16), pltpu.VMEM((bb * tt, D), F32)],
        compiler_params=_cparams(("parallel", "parallel", "arbitrary")),
        name="ffn",
    )(x, s, c, g, ng_pre, ng_post, w_in, w_in, w_out)


def _rwkv_pre_kernel(x_ref, s_ref, c_ref, ng_ref, shift_ref, mu_ref, wrkv_ref, w0_ref, w1_ref, w2_ref,
                     a0_ref, a1_ref, a2_ref, g1_ref, g2_ref,
                     r_ref, k_ref, v_ref, a_ref, lw_ref, g_ref, shift_out_ref, prev_scr):
    bb, tt, D = x_ref.shape

    @pl.when(pl.program_id(1) == 0)
    def _():
        prev_scr[...] = shift_ref[...]

    hm = _rms(x_ref[...], ng_ref[...]) * (1.0 + c_ref[...]) + s_ref[...]
    row = lax.broadcasted_iota(jnp.int32, hm.shape, 1)
    xprev = jnp.where(row == 0, prev_scr[...], pltpu.roll(hm, 1, axis=1))
    last = hm[:, tt - 1:tt, :]
    prev_scr[...] = last
    shift_out_ref[...] = last
    dx = xprev - hm

    def mix(j):
        return (hm + dx * mu_ref[j:j + 1, :]).reshape(bb * tt, D).astype(BF16)

    def out(ref, val):
        ref[...] = val.reshape(bb, tt, D)

    out(r_ref, _dot(mix(0), wrkv_ref[0]))
    out(k_ref, _dot(mix(2), wrkv_ref[1]))
    out(v_ref, _dot(mix(3), wrkv_ref[2]))
    z = w0_ref[...] + _dot(jnp.tanh(_dot(mix(1), w1_ref[...])).astype(BF16), w2_ref[...])
    nz = -z
    softplus = jnp.maximum(nz, 0.0) + jnp.log1p(jnp.exp(-jnp.abs(nz)))
    out(lw_ref, -jnp.exp(-softplus - 0.5))
    out(a_ref, _sigmoid(a0_ref[...] + _dot(_dot(mix(4), a1_ref[...]).astype(BF16), a2_ref[...])))
    out(g_ref, _dot(_sigmoid(_dot(mix(5), g1_ref[...])).astype(BF16), g2_ref[...]))


def _rwkv_pre(x, s, c, ng, shift_in, mu, wrkv, w0, w1, w2, a0, a1, a2, g1, g2, rows=256):
    B, T, D = x.shape
    bb, tt = _tile(B, T, rows)
    xspec = pl.BlockSpec((bb, tt, D), lambda b, i: (b, i, 0))
    mspec = pl.BlockSpec((bb, 1, D), lambda b, i: (b, 0, 0))

    def full(a):
        return pl.BlockSpec(a.shape, lambda b, i: (0,) * a.ndim)

    big = jax.ShapeDtypeStruct((B, T, D), F32)
    return pl.pallas_call(
        _rwkv_pre_kernel,
        grid=(B // bb, T // tt),
        in_specs=[xspec, mspec, mspec, full(ng), mspec, full(mu), full(wrkv), full(w0), full(w1), full(w2),
                  full(a0), full(a1), full(a2), full(g1), full(g2)],
        out_specs=[xspec] * 6 + [mspec],
        out_shape=[big] * 6 + [jax.ShapeDtypeStruct((B, 1, D), F32)],
        scratch_shapes=[pltpu.VMEM((bb, 1, D), F32)],
        compiler_params=_cparams(("parallel", "arbitrary")),
        name="rwkv_pre",
    )(x, s, c, ng, shift_in, mu, wrkv, w0, w1, w2, a0, a1, a2, g1, g2)


def _pieces(x, n):
    out = []
    for _ in range(n):
        p = x.astype(BF16)
        out.append(p)
        x = x - p.astype(F32)
    return out


def _pdot(a, b, dims):
    order = max(len(a), len(b))
    acc = None
    for i, ai in enumerate(a):
        for j, bj in enumerate(b):
            if i + j < order:
                t = lax.dot_general(ai, bj, (dims, ((), ())), preferred_element_type=F32)
                acc = t if acc is None else acc + t
    return acc


_NN = ((1,), (0,))
_NT = ((1,), (1,))
_TN = ((0,), (0,))

WKV_P_IN = 1
WKV_P_TRI = 1


def _wkv_kernel(r_ref, k_ref, v_ref, a_ref, lw_ref, g_ref, s0_ref, kk_ref, ka_ref, rk_ref, lnw_ref, lnb_ref,
                o_ref, sout_ref, s_scr):
    C = CHUNK
    L2 = 2 * LANES
    npairs = s_scr.shape[0]

    @pl.when(pl.program_id(1) == 0)
    def _():
        s_scr[...] = s0_ref[...]

    ri = lax.broadcasted_iota(jnp.int32, (LANES, L2), 0)
    ci = lax.broadcasted_iota(jnp.int32, (LANES, L2), 1) % LANES
    same_head = (ri // C) == (ci // C)
    tril_incl = same_head & ((ci % C) <= (ri % C))
    tril_strict = same_head & ((ci % C) < (ri % C))
    ri1 = lax.broadcasted_iota(jnp.int32, (LANES, LANES), 0)
    ci1 = lax.broadcasted_iota(jnp.int32, (LANES, LANES), 1)
    head_ones = [((ri1 // C) == (ci1 // C)).astype(F32).astype(BF16)]
    eye = (ri1 == ci1).astype(F32)
    r64 = lax.broadcasted_iota(jnp.int32, (C, C), 0)
    c64 = lax.broadcasted_iota(jnp.int32, (C, C), 1)
    cumsum_mat = [(c64 <= r64).astype(F32).astype(BF16)]
    lane = lax.broadcasted_iota(jnp.int32, (C, LANES), 1)
    first_head = lane < RWKV_HEAD

    def stack(x):
        return jnp.concatenate([jnp.where(first_head, x, 0.0), jnp.where(first_head, 0.0, x)], axis=0)

    def level_mask(b):
        return ((ri1 // (2 * b)) == (ci1 // (2 * b))) & ((ri1 % (2 * b)) >= b) & ((ci1 % (2 * b)) < b)

    pairs = range(npairs)

    def pair(x, p):
        return x[:, p * LANES:(p + 1) * LANES]

    def head_sum(x):
        xs = jnp.concatenate([pair(x, p) for p in pairs], axis=0)
        ys = _pdot(_pieces(xs, 2), head_ones, _NN)
        return jnp.concatenate([ys[p * C:(p + 1) * C] for p in pairs], axis=1)

    r, k, v, a, lw = r_ref[...], k_ref[...], v_ref[...], a_ref[...], lw_ref[...]
    kk = k * kk_ref[...]
    kk = kk / jnp.maximum(jnp.sqrt(head_sum(kk * kk)), 1e-12)
    kmod = k * (1.0 + (a - 1.0) * ka_ref[...])
    bvec = kk * a
    cum = _pdot(cumsum_mat, _pieces(lw, 3), _NN)
    total = cum[C - 1:C, :]
    inv = jnp.exp(-cum)
    rem = jnp.exp(total - cum)
    decay = jnp.exp(total)
    a_t, r_t = -kk * jnp.exp(cum - lw), r * jnp.exp(cum)
    b_t, k_t = bvec * inv, kmod * inv
    b_h, k_h = bvec * rem, kmod * rem

    def stack2(x, y, p):
        return _pieces(jnp.concatenate([stack(pair(x, p)), stack(pair(y, p))], axis=0), WKV_P_IN)

    ar2 = [stack2(a_t, r_t, p) for p in pairs]
    bk2 = [stack2(b_t, k_t, p) for p in pairs]
    bkh2 = [stack2(b_h, k_h, p) for p in pairs]
    v2 = [stack(pair(v, p)) for p in pairs]

    gram = [_pdot(ar2[p], bk2[p], _NT) for p in pairs]
    lab_lak = [jnp.where(tril_strict, gram[p][:LANES], 0.0) for p in pairs]
    lab = [t[:, :LANES] for t in lab_lak]
    lak = [_pieces(t[:, LANES:], WKV_P_TRI) for t in lab_lak]
    mrbk = [_pieces(jnp.where(tril_incl, gram[p][LANES:], 0.0), WKV_P_TRI) for p in pairs]

    x = [eye + jnp.where(level_mask(1), lab[p], 0.0) for p in pairs]
    for b in (2, 4, 8, 16, 32):
        mask = level_mask(b)
        xp = [_pieces(x[p], WKV_P_TRI) for p in pairs]
        t = [_pdot(xp[p], _pieces(jnp.where(mask, lab[p], 0.0), WKV_P_TRI), _NN) for p in pairs]
        x = [x[p] + _pdot(_pieces(t[p], WKV_P_TRI), xp[p], _NN) for p in pairs]

    s = [s_scr[p] for p in pairs]
    ars = [_pdot(ar2[p], _pieces(s[p], WKV_P_IN), _NT) for p in pairs]
    rhs = [ars[p][:LANES] + _pdot(lak[p], _pieces(v2[p], WKV_P_IN), _NN) for p in pairs]
    u2 = [_pdot(_pieces(x[p], WKV_P_TRI), _pieces(rhs[p], WKV_P_IN), _NN) for p in pairs]
    uv = [_pieces(jnp.concatenate([u2[p], v2[p]], axis=0), WKV_P_IN) for p in pairs]
    y2 = [ars[p][LANES:] + _pdot(mrbk[p], uv[p], _NN) for p in pairs]
    for p in pairs:
        s_scr[p] = s[p] * pair(decay, p) + _pdot(uv[p], bkh2[p], _TN)
    y = jnp.concatenate([t[:C] + t[C:] for t in y2], axis=1)

    mean = head_sum(y) * (1.0 / RWKV_HEAD)
    d = y - mean
    var = head_sum(d * d) * (1.0 / RWKV_HEAD)
    yn = d * lax.rsqrt(var + GN_EPS) * lnw_ref[...] + lnb_ref[...]
    bonus = head_sum(r * kmod * rk_ref[...]) * v
    o_ref[...] = (yn + bonus) * g_ref[...]

    @pl.when(pl.program_id(1) == pl.num_programs(1) - 1)
    def _():
        sout_ref[...] = s_scr[...]


def _wkv(r, k, v, a, lw, g, s0_bd, k_k, k_a, r_k, ln_w, ln_b):
    B, T, D = r.shape
    npairs = D // LANES
    xspec = pl.BlockSpec((None, CHUNK, D), lambda b, i: (b, i, 0))
    sspec = pl.BlockSpec((None, npairs, LANES, LANES), lambda b, i: (b, 0, 0, 0))
    pspec = pl.BlockSpec((1, D), lambda b, i: (0, 0))
    return pl.pallas_call(
        _wkv_kernel,
        grid=(B, T // CHUNK),
        in_specs=[xspec] * 6 + [sspec] + [pspec] * 5,
        out_specs=[xspec, sspec],
        out_shape=[jax.ShapeDtypeStruct((B, T, D), F32),
                   jax.ShapeDtypeStruct((B, npairs, LANES, LANES), F32)],
        scratch_shapes=[pltpu.VMEM((npairs, LANES, LANES), F32)],
        compiler_params=_cparams(("parallel", "arbitrary")),
        name="wkv",
    )(r, k, v, a, lw, g, s0_bd, k_k, k_a, r_k, ln_w, ln_b)


def _state_to_blockdiag(s):
    B, H, N, _ = s.shape
    s = s.reshape(B, H // 2, 2, N, N)
    z = jnp.zeros_like(s[:, :, 0])
    top = jnp.concatenate([s[:, :, 0], z], axis=-1)
    bot = jnp.concatenate([z, s[:, :, 1]], axis=-1)
    return jnp.concatenate([top, bot], axis=-2)


def _state_from_blockdiag(s):
    B, P, N2, _ = s.shape
    N = N2 // 2
    return jnp.stack([s[:, :, :N, :N], s[:, :, N:, N:]], axis=2).reshape(B, 2 * P, N, N)


def _proj_res_kernel(x_ref, a_ref, g_ref, ng_ref, w_ref, o_ref):
    bb, tt, D = x_ref.shape
    a = a_ref[...].reshape(bb * tt, a_ref.shape[-1]).astype(BF16)
    out = _dot(a, w_ref[...]).reshape(bb, tt, D)
    o_ref[...] = x_ref[...] + g_ref[...] * _rms(out, ng_ref[...])


def _proj_res(x, a, g, ng, w, rows=512):
    B, T, D = x.shape
    bb, tt = _tile(B, T, rows)
    xspec = pl.BlockSpec((bb, tt, D), lambda b, i: (b, i, 0))
    return pl.pallas_call(
        _proj_res_kernel,
        grid=(B // bb, T // tt),
        in_specs=[xspec, pl.BlockSpec((bb, tt, a.shape[-1]), lambda b, i: (b, i, 0)),
                  pl.BlockSpec((bb, 1, D), lambda b, i: (b, 0, 0)),
                  pl.BlockSpec((1, D), lambda b, i: (0, 0)),
                  pl.BlockSpec(w.shape, lambda b, i: (0, 0))],
        out_specs=xspec,
        out_shape=jax.ShapeDtypeStruct((B, T, D), F32),
        compiler_params=_cparams(("parallel", "parallel")),
        name="proj_res",
    )(x, a, g, ng, w)


def _rope_tables(pos):
    half = ROT_DIM // 2
    inv = ROPE_THETA ** (-jnp.arange(half, dtype=F32) * 2.0 / ROT_DIM)
    ang = pos.astype(F32)[:, None] * inv[None, :]
    cos, sin = jnp.cos(ang), jnp.sin(ang)
    T = pos.shape[0]
    ones = jnp.ones((T, HEAD_QK - ROT_DIM), F32)
    zeros = jnp.zeros((T, HEAD_QK - ROT_DIM), F32)
    zh = jnp.zeros((T, half), F32)
    c = jnp.concatenate([cos, cos, ones], axis=-1)
    s1 = jnp.concatenate([-sin, zh, zeros], axis=-1)
    s2 = jnp.concatenate([zh, sin, zeros], axis=-1)
    return tuple(jnp.concatenate([t, t], axis=-1) for t in (c, s1, s2))


def _rope_rows(y, c, s1, s2):
    half = ROT_DIM // 2
    outs = []
    for p in range(y.shape[-1] // LANES):
        xs = y[:, p * LANES:(p + 1) * LANES]
        outs.append(xs * c + pltpu.roll(xs, LANES - half, axis=1) * s1 + pltpu.roll(xs, half, axis=1) * s2)
    return jnp.concatenate(outs, axis=-1)


def _nmm_kv_kernel(x_ref, s_ref, c_ref, ng_ref, w_ref, tc_ref, ts1_ref, ts2_ref,
                   k_ref, v_ref, kb_ref, vb_ref):
    bb, tt, D = x_ref.shape
    h = _rms(x_ref[...], ng_ref[...]) * (1.0 + c_ref[...]) + s_ref[...]
    kv = _dot(h.reshape(bb * tt, D).astype(BF16), w_ref[...])
    nk = k_ref.shape[-1]
    k = _rope_rows(kv[:, :nk], tc_ref[...], ts1_ref[...], ts2_ref[...]).reshape(bb, tt, nk)
    v = kv[:, nk:].reshape(bb, tt, v_ref.shape[-1])
    k_ref[...] = k
    v_ref[...] = v
    kb_ref[...] = k.astype(BF16)
    vb_ref[...] = v.astype(BF16)


def _nmm_q_kernel(x_ref, s_ref, c_ref, ng_ref, w_ref, tc_ref, ts1_ref, ts2_ref, q_ref):
    bb, tt, D = x_ref.shape
    h = _rms(x_ref[...], ng_ref[...]) * (1.0 + c_ref[...]) + s_ref[...]
    q = _dot(h.reshape(bb * tt, D).astype(BF16), w_ref[...])
    q = _rope_rows(q, tc_ref[...], ts1_ref[...], ts2_ref[...]) * (ATTN_SCALE * LOG2E)
    q_ref[...] = q.reshape(bb, tt, q_ref.shape[-1]).astype(BF16)


def _nmm_rope(body, x, s, c, ng, w, pos, out_widths, out_dtypes, rows=512):
    B, T, D = x.shape
    bb, tt = _tile(B, T, rows)
    tabs = _rope_tables(pos)
    if bb > 1:
        tabs = tuple(jnp.tile(t, (bb, 1)) for t in tabs)
        tspec = pl.BlockSpec((bb * tt, LANES), lambda b, i: (0, 0))
    else:
        tspec = pl.BlockSpec((tt, LANES), lambda b, i: (i, 0))
    xspec = pl.BlockSpec((bb, tt, D), lambda b, i: (b, i, 0))
    mspec = pl.BlockSpec((bb, 1, D), lambda b, i: (b, 0, 0))
    return pl.pallas_call(
        body,
        grid=(B // bb, T // tt),
        in_specs=[xspec, mspec, mspec, pl.BlockSpec((1, D), lambda b, i: (0, 0)),
                  pl.BlockSpec(w.shape, lambda b, i: (0, 0)), tspec, tspec, tspec],
        out_specs=[pl.BlockSpec((bb, tt, n), lambda b, i: (b, i, 0)) for n in out_widths],
        out_shape=[jax.ShapeDtypeStruct((B, T, n), dt) for n, dt in zip(out_widths, out_dtypes)],
        compiler_params=_cparams(("parallel", "parallel")),
        name="nmm_rope",
    )(x, s, c, ng, w, *tabs)


def _attn_kernel(q_ref, k_ref, v_ref, lam_ref, sg_ref, o_ref, m_scr, acc_scr, *, q_off, tk, lam_init):
    tq = q_ref.shape[0]
    qi = pl.program_id(2)
    q0 = q_off + qi * tq
    lane = lax.broadcasted_iota(jnp.int32, (tq, LANES), 1)
    q = q_ref[...]
    zero = jnp.zeros_like(q)
    qs = jnp.concatenate([jnp.where(lane < HEAD_QK, q, zero), jnp.where(lane < HEAD_QK, zero, q)], axis=0)

    m_scr[...] = jnp.full_like(m_scr, NEG_INF)
    acc_scr[...] = jnp.zeros_like(acc_scr)

    def update(start, size, mask):
        kt = k_ref[pl.ds(start, size), :]
        vt = jnp.concatenate([v_ref[pl.ds(start, size), :], jnp.ones((size, LANES), BF16)], axis=1)
        s = lax.dot_general(qs, kt, (((1,), (1,)), ((), ())), preferred_element_type=F32)
        if mask is not None:
            s = jnp.where(mask, s, NEG_INF)
        m_old = m_scr[...]
        m_new = jnp.maximum(m_old, jnp.max(s, axis=-1, keepdims=True))
        m_wide = m_new[:, :size] if size < LANES else jnp.concatenate([m_new] * (size // LANES), axis=1)
        p = jnp.exp2(s - m_wide)
        alpha = jnp.exp2(m_old - m_new)
        acc_scr[...] = jnp.concatenate([alpha, alpha], axis=1) * acc_scr[...] + _dot(p.astype(BF16), vt)
        m_scr[...] = m_new

    def full_tile(j, carry):
        update(pl.multiple_of(j * tk, tk), tk, None)
        return carry

    def part_tile(j, carry):
        update(pl.multiple_of((q0 // tk) * tk + j * tq, tq), tq, None)
        return carry

    lax.fori_loop(0, q0 // tk, full_tile, 0)
    lax.fori_loop(0, (q0 % tk) // tq, part_tile, 0)

    row = lax.broadcasted_iota(jnp.int32, (2 * tq, tq), 0)
    col = lax.broadcasted_iota(jnp.int32, (2 * tq, tq), 1)
    visible = (col // CHUNK) <= ((row % tq) // CHUNK)
    update(pl.multiple_of(q0, CHUNK), tq, visible)

    lp = lam_ref[...]
    lam = (jnp.exp(jnp.sum(lp[0:1] * lp[1:2], axis=-1, keepdims=True))
           - jnp.exp(jnp.sum(lp[2:3] * lp[3:4], axis=-1, keepdims=True)) + lam_init)
    acc = acc_scr[...]
    o = acc[:, :LANES] / acc[:, LANES:]
    o = o[:tq] - lam * o[tq:]
    o = o * lax.rsqrt(jnp.mean(o * o, axis=-1, keepdims=True) + NORM_EPS)
    o_ref[...] = o * sg_ref[...] * (1.0 - lam_init)


def _attn(q, kb, vb, lam_p, subln_g, q_off, lam_init, tq=512, tk=1024):
    B, T, W = q.shape
    Tk = kb.shape[1]
    H = W // LANES
    tq = min(T, tq)
    assert T % tq == 0 and q_off % tk == 0 and (tk % tq == 0 or T == tq) and Tk == q_off + T
    qspec = pl.BlockSpec((None, tq, LANES), lambda b, h, i: (b, i, h))
    kspec = pl.BlockSpec((None, Tk, LANES), lambda b, h, i: (b, 0, h))
    return pl.pallas_call(
        functools.partial(_attn_kernel, q_off=q_off, tk=tk, lam_init=lam_init),
        grid=(B, H, T // tq),
        in_specs=[qspec, kspec, kspec,
                  pl.BlockSpec(lam_p.shape, lambda b, h, i: (0, 0)),
                  pl.BlockSpec((1, LANES), lambda b, h, i: (0, 0))],
        out_specs=qspec,
        out_shape=jax.ShapeDtypeStruct((B, T, W), F32),
        scratch_shapes=[pltpu.VMEM((2 * tq, LANES), F32), pltpu.VMEM((2 * tq, 2 * LANES), F32)],
        compiler_params=_cparams(("parallel", "parallel", "arbitrary")),
        name="attn",
    )(q, kb, vb, lam_p, subln_g)


def _trunk(x, mods, kv_mod, past_len, wkv0, shift0, k_past, v_past, P):
    B, T, D = x.shape
    pos = past_len + jnp.arange(T)

    def chunks(m, n):
        return [m[:, None, j * D:(j + 1) * D] for j in range(n)]

    row = lambda a: a.reshape(1, -1)
    for l in range(2):
        s1, c1, g1, s2, c2, g2, s3, c3, g3 = chunks(mods[l], 9)
        ng = P['norm_g'][l]
        if l == 1:
            kv_shift, kv_scale = chunks(kv_mod, 2)
            k_new, v_new, kb, vb = _nmm_rope(_nmm_kv_kernel, x, kv_shift, kv_scale, row(P['kv_norm_g']),
                                             P['kv_w'], pos, (D, D, D, D), (F32, F32, BF16, BF16))
            if k_past is not None:
                kb = jnp.concatenate([k_past.reshape(B, past_len, D).astype(BF16), kb], axis=1)
                vb = jnp.concatenate([v_past.reshape(B, past_len, D).astype(BF16), vb], axis=1)
        x = _ffn(x, s1, c1, g1, row(ng[0]), row(ng[1]), P['ffn_w_in'][l][0], P['ffn_w_out'][l][0])
        if l == 0:
            r, k, v, a, lw, g, shift = _rwkv_pre(
                x, s2, c2, row(ng[2]), shift0, P['rwkv_mu'], P['rwkv_w_rkv'], row(P['rwkv_w0']),
                P['rwkv_w1'], P['rwkv_w2'], row(P['rwkv_a0']), P['rwkv_a1'], P['rwkv_a2'],
                P['rwkv_g1'], P['rwkv_g2'])
            o, s_bd = _wkv(r, k, v, a, lw, g, _state_to_blockdiag(wkv0), row(P['rwkv_k_k']),
                           row(P['rwkv_k_a']), row(P['rwkv_r_k']), row(P['rwkv_ln_w']), row(P['rwkv_ln_b']))
            wkv = _state_from_blockdiag(s_bd)
            x = _proj_res(x, o, g2, row(ng[3]), P['rwkv_w_o'])
        else:
            (q,) = _nmm_rope(_nmm_q_kernel, x, s2, c2, row(ng[2]), P['diff_w_q'], pos, (D,), (BF16,))
            lam_init = 0.8 - 0.6 * math.exp(-0.3 * l)
            o = _attn(q, kb, vb, P['diff_lambda'], row(P['diff_subln_g']), past_len, lam_init)
            x = _proj_res(x, o, g2, row(ng[3]), P['diff_w_o'])
        x = _ffn(x, s3, c3, g3, row(ng[4]), row(ng[5]), P['ffn_w_in'][l][1], P['ffn_w_out'][l][1])
    return (x, wkv[None], shift[None],
            k_new.reshape(B, T, N_HEADS_B, 2, HEAD_QK), v_new.reshape(B, T, N_HEADS_B, HEAD_V))


def kernel(x_prompt, x_sample, c_prompt, c_sample, state_wkv, state_shift, cache_k, cache_v, ada_w, ada_b, norm_g, ffn_w_in, ffn_w_out, rwkv_mu, rwkv_w_rkv, rwkv_w0, rwkv_w1, rwkv_w2, rwkv_a0, rwkv_a1, rwkv_a2, rwkv_g1, rwkv_g2, rwkv_k_k, rwkv_k_a, rwkv_r_k, rwkv_ln_w, rwkv_ln_b, rwkv_w_o, kv_ada_w, kv_ada_b, kv_norm_g, kv_w, diff_w_q, diff_lambda, diff_subln_g, diff_w_o):
    Bp, _, D = x_prompt.shape
    Bs = x_sample.shape[0]
    H = D // RWKV_HEAD
    bf = lambda w: w.astype(BF16)
    P = dict(norm_g=norm_g, ffn_w_in=bf(ffn_w_in), ffn_w_out=bf(ffn_w_out),
             rwkv_mu=rwkv_mu[0], rwkv_w_rkv=bf(rwkv_w_rkv[0]), rwkv_w0=rwkv_w0[0], rwkv_w1=bf(rwkv_w1[0]),
             rwkv_w2=bf(rwkv_w2[0]), rwkv_a0=rwkv_a0[0], rwkv_a1=bf(rwkv_a1[0]), rwkv_a2=bf(rwkv_a2[0]),
             rwkv_g1=bf(rwkv_g1[0]), rwkv_g2=bf(rwkv_g2[0]), rwkv_k_k=rwkv_k_k[0], rwkv_k_a=rwkv_k_a[0],
             rwkv_r_k=rwkv_r_k[0], rwkv_ln_w=rwkv_ln_w[0], rwkv_ln_b=rwkv_ln_b[0], rwkv_w_o=bf(rwkv_w_o[0]),
             kv_norm_g=kv_norm_g, kv_w=bf(kv_w), diff_w_q=bf(diff_w_q[0]), diff_lambda=diff_lambda[0],
             diff_subln_g=diff_subln_g[0], diff_w_o=bf(diff_w_o[0]))

    c_all = jnp.concatenate([c_prompt, c_sample], axis=0)
    mods = _ada(c_all, ada_w, ada_b[:, None, :])
    kv_mod = _ada(c_all, kv_ada_w[None], kv_ada_b[None, None, :])[0]

    wkv0 = jnp.zeros((Bp, H, RWKV_HEAD, RWKV_HEAD), F32)
    shift0 = jnp.zeros((Bp, 1, D), F32)
    y_p, wkv_p, shift_p, k_p, v_p = _trunk(
        x_prompt, [mods[0, :Bp], mods[1, :Bp]], kv_mod[:Bp], 0, wkv0, shift0, None, None, P)
    y_s, wkv_s, shift_s, k_s, v_s = _trunk(
        x_sample, [mods[0, Bp:], mods[1, Bp:]], kv_mod[Bp:], cache_k.shape[1],
        state_wkv[0], state_shift[0], cache_k, cache_v, P)
    return (y_p, y_s, wkv_p, shift_p, k_p, v_p, wkv_s, shift_s, k_s, v_s)
```

```python
import functools
import math

import jax
import jax.numpy as jnp
from jax import lax
from jax.experimental import pallas as pl
from jax.experimental.pallas import tpu as pltpu

F32 = jnp.float32
BF16 = jnp.bfloat16
HI = lax.Precision.HIGHEST

CHUNK = 64
RWKV_HEAD = 64
GN_EPS = 64e-5
N_HEADS_B = 8
HEAD_QK = 64
HEAD_V = 128
ROT_DIM = HEAD_QK // 4
ROPE_THETA = 500000.0
ATTN_SCALE = HEAD_QK ** -0.5
LOG2E = math.log2(math.e)
NORM_EPS = 1e-6
NEG_INF = -1e30

LANES = 128
VMEM_LIMIT = 56 * 1024 * 1024


def _cparams(sem):
    return pltpu.CompilerParams(dimension_semantics=sem, vmem_limit_bytes=VMEM_LIMIT)


def _tile(B, T, rows):
    if T >= rows:
        assert T % rows == 0
        return 1, rows
    bb = min(B, max(1, rows // T))
    while B % bb:
        bb -= 1
    return bb, T


def _rms(x, g):
    return x * lax.rsqrt(jnp.mean(x * x, axis=-1, keepdims=True) + NORM_EPS) * g


def _sigmoid(x):
    return 1.0 / (1.0 + jnp.exp(-x))


def _dot(a, b):
    return jnp.dot(a, b, preferred_element_type=F32)


def _dot_hi(a, b):
    return jnp.dot(a, b, preferred_element_type=F32, precision=HI)


def _dot_nt_hi(a, b):
    return lax.dot_general(a, b, (((1,), (1,)), ((), ())), preferred_element_type=F32, precision=HI)


def _dot_tn_hi(a, b):
    return lax.dot_general(a, b, (((0,), (0,)), ((), ())), preferred_element_type=F32, precision=HI)


def _ada_kernel(c_ref, w_ref, b_ref, o_ref):
    c = c_ref[...]
    sc = (c * _sigmoid(c)).astype(BF16)
    o_ref[...] = _dot(sc, w_ref[...].astype(BF16)) + b_ref[...]


def _ada(c, w, b, tn=1024):
    L, D, N = w.shape
    M = c.shape[0]
    return pl.pallas_call(
        _ada_kernel,
        grid=(L, N // tn),
        in_specs=[pl.BlockSpec((M, D), lambda l, j: (0, 0)),
                  pl.BlockSpec((None, D, tn), lambda l, j: (l, 0, j)),
                  pl.BlockSpec((None, 1, tn), lambda l, j: (l, 0, j))],
        out_specs=pl.BlockSpec((None, M, tn), lambda l, j: (l, 0, j)),
        out_shape=jax.ShapeDtypeStruct((L, M, N), F32),
        compiler_params=_cparams(("parallel", "parallel")),
        name="ada",
    )(c, w, b)


def _ffn_kernel(x_ref, s_ref, c_ref, g_ref, ngpre_ref, ngpost_ref, wg_ref, wu_ref, wo_ref, o_ref, *, tf):
    bb, tt, D = x_ref.shape
    x = x_ref[...]
    h = (_rms(x, ngpre_ref[...]) * (1.0 + c_ref[...]) + s_ref[...]).reshape(bb * tt, D).astype(BF16)
    acc = None
    for j in range(wo_ref.shape[0] // tf):
        cols = slice(j * tf, (j + 1) * tf)
        gate = _dot(h, wg_ref[:, cols])
        up = _dot(h, wu_ref[:, cols])
        act = ((gate * _sigmoid(gate)) * up).astype(BF16)
        part = _dot(act, wo_ref[cols, :])
        acc = part if acc is None else acc + part
    o_ref[...] = x + 0.5 * g_ref[...] * _rms(acc.reshape(bb, tt, D), ngpost_ref[...])


def _ffn(x, s, c, g, ng_pre, ng_post, w_in, w_out, l, m, rows=512, tf=256):
    B, T, D = x.shape
    Fh = w_out.shape[2]
    bb, tt = _tile(B, T, rows)
    xspec = pl.BlockSpec((bb, tt, D), lambda b, i: (b, i, 0))
    mspec = pl.BlockSpec((bb, 1, D), lambda b, i: (b, 0, 0))
    gspec = pl.BlockSpec((1, D), lambda b, i: (0, 0))
    once = pl.Buffered(1)
    return pl.pallas_call(
        functools.partial(_ffn_kernel, tf=tf),
        grid=(B // bb, T // tt),
        in_specs=[xspec, mspec, mspec, mspec, gspec, gspec,
                  pl.BlockSpec((None, None, D, Fh), lambda b, i: (l, m, 0, 0), pipeline_mode=once),
                  pl.BlockSpec((None, None, D, Fh), lambda b, i: (l, m, 0, 1), pipeline_mode=once),
                  pl.BlockSpec((None, None, Fh, D), lambda b, i: (l, m, 0, 0), pipeline_mode=once)],
        out_specs=xspec,
        out_shape=jax.ShapeDtypeStruct((B, T, D), F32),
        compiler_params=_cparams(("parallel", "parallel")),
        name="ffn",
    )(x, s, c, g, ng_pre, ng_post, w_in, w_in, w_out)


def _rwkv_pre_kernel(x_ref, s_ref, c_ref, ng_ref, shift_ref, mu_ref, wrkv_ref, w0_ref, w1_ref, w2_ref,
                     a0_ref, a1_ref, a2_ref, g1_ref, g2_ref,
                     r_ref, k_ref, v_ref, a_ref, lw_ref, g_ref, shift_out_ref, prev_scr):
    bb, tt, D = x_ref.shape

    @pl.when(pl.program_id(1) == 0)
    def _():
        prev_scr[...] = shift_ref[...]

    hm = _rms(x_ref[...], ng_ref[...]) * (1.0 + c_ref[...]) + s_ref[...]
    row = lax.broadcasted_iota(jnp.int32, hm.shape, 1)
    xprev = jnp.where(row == 0, prev_scr[...], pltpu.roll(hm, 1, axis=1))
    last = hm[:, tt - 1:tt, :]
    prev_scr[...] = last
    shift_out_ref[...] = last
    dx = xprev - hm

    def mix(j):
        return (hm + dx * mu_ref[j:j + 1, :]).reshape(bb * tt, D).astype(BF16)

    def out(ref, val):
        ref[...] = val.reshape(bb, tt, D)

    out(r_ref, _dot(mix(0), wrkv_ref[0]))
    out(k_ref, _dot(mix(2), wrkv_ref[1]))
    out(v_ref, _dot(mix(3), wrkv_ref[2]))
    z = w0_ref[...] + _dot(jnp.tanh(_dot(mix(1), w1_ref[...])).astype(BF16), w2_ref[...])
    nz = -z
    softplus = jnp.maximum(nz, 0.0) + jnp.log1p(jnp.exp(-jnp.abs(nz)))
    out(lw_ref, -jnp.exp(-softplus - 0.5))
    out(a_ref, _sigmoid(a0_ref[...] + _dot(_dot(mix(4), a1_ref[...]).astype(BF16), a2_ref[...])))
    out(g_ref, _dot(_sigmoid(_dot(mix(5), g1_ref[...])).astype(BF16), g2_ref[...]))


def _rwkv_pre(x, s, c, ng, shift_in, mu, wrkv, w0, w1, w2, a0, a1, a2, g1, g2, rows=256):
    B, T, D = x.shape
    bb, tt = _tile(B, T, rows)
    xspec = pl.BlockSpec((bb, tt, D), lambda b, i: (b, i, 0))
    mspec = pl.BlockSpec((bb, 1, D), lambda b, i: (b, 0, 0))

    def full(a):
        return pl.BlockSpec(a.shape, lambda b, i: (0,) * a.ndim)

    big = jax.ShapeDtypeStruct((B, T, D), F32)
    return pl.pallas_call(
        _rwkv_pre_kernel,
        grid=(B // bb, T // tt),
        in_specs=[xspec, mspec, mspec, full(ng), mspec, full(mu), full(wrkv), full(w0), full(w1), full(w2),
                  full(a0), full(a1), full(a2), full(g1), full(g2)],
        out_specs=[xspec] * 6 + [mspec],
        out_shape=[big] * 6 + [jax.ShapeDtypeStruct((B, 1, D), F32)],
        scratch_shapes=[pltpu.VMEM((bb, 1, D), F32)],
        compiler_params=_cparams(("parallel", "arbitrary")),
        name="rwkv_pre",
    )(x, s, c, ng, shift_in, mu, wrkv, w0, w1, w2, a0, a1, a2, g1, g2)


def _pieces(x, n):
    out = []
    for _ in range(n):
        p = x.astype(BF16)
        out.append(p)
        x = x - p.astype(F32)
    return out


def _pdot(a, b, dims):
    order = max(len(a), len(b))
    acc = None
    for i, ai in enumerate(a):
        for j, bj in enumerate(b):
            if i + j < order:
                t = lax.dot_general(ai, bj, (dims, ((), ())), preferred_element_type=F32)
                acc = t if acc is None else acc + t
    return acc


_NN = ((1,), (0,))
_NT = ((1,), (1,))
_TN = ((0,), (0,))

WKV_P_IN = 1
WKV_P_TRI = 1


def _wkv_kernel(r_ref, k_ref, v_ref, a_ref, lw_ref, g_ref, s0_ref, kk_ref, ka_ref, rk_ref, lnw_ref, lnb_ref,
                o_ref, s_ref):
    C = CHUNK
    L2 = 2 * LANES
    npairs = s_ref.shape[0]

    @pl.when(pl.program_id(1) == 0)
    def _():
        s_ref[...] = s0_ref[...]

    ri = lax.broadcasted_iota(jnp.int32, (LANES, L2), 0)
    ci = lax.broadcasted_iota(jnp.int32, (LANES, L2), 1) % LANES
    same_head = (ri // C) == (ci // C)
    tril_incl = same_head & ((ci % C) <= (ri % C))
    tril_strict = same_head & ((ci % C) < (ri % C))
    ri1 = lax.broadcasted_iota(jnp.int32, (LANES, LANES), 0)
    ci1 = lax.broadcasted_iota(jnp.int32, (LANES, LANES), 1)
    head_ones = [((ri1 // C) == (ci1 // C)).astype(F32).astype(BF16)]
    eye = (ri1 == ci1).astype(F32)
    r64 = lax.broadcasted_iota(jnp.int32, (C, C), 0)
    c64 = lax.broadcasted_iota(jnp.int32, (C, C), 1)
    cumsum_mat = [(c64 <= r64).astype(F32).astype(BF16)]
    lane = lax.broadcasted_iota(jnp.int32, (C, LANES), 1)
    first_head = lane < RWKV_HEAD

    def stack(x):
        return jnp.concatenate([jnp.where(first_head, x, 0.0), jnp.where(first_head, 0.0, x)], axis=0)

    def level_mask(b):
        return ((ri1 // (2 * b)) == (ci1 // (2 * b))) & ((ri1 % (2 * b)) >= b) & ((ci1 % (2 * b)) < b)

    pairs = range(npairs)

    def pair(x, p):
        return x[:, p * LANES:(p + 1) * LANES]

    def head_sum(x):
        xs = jnp.concatenate([pair(x, p) for p in pairs], axis=0)
        ys = _pdot(_pieces(xs, 2), head_ones, _NN)
        return jnp.concatenate([ys[p * C:(p + 1) * C] for p in pairs], axis=1)

    r, k, v, a, lw = r_ref[...], k_ref[...], v_ref[...], a_ref[...], lw_ref[...]
    kk = k * kk_ref[...]
    kk = kk / jnp.maximum(jnp.sqrt(head_sum(kk * kk)), 1e-12)
    kmod = k * (1.0 + (a - 1.0) * ka_ref[...])
    bvec = kk * a
    cum = _pdot(cumsum_mat, _pieces(lw, 3), _NN)
    total = cum[C - 1:C, :]
    inv = jnp.exp(-cum)
    rem = jnp.exp(total - cum)
    decay = jnp.exp(total)
    a_t, r_t = -kk * jnp.exp(cum - lw), r * jnp.exp(cum)
    b_t, k_t = bvec * inv, kmod * inv
    b_h, k_h = bvec * rem, kmod * rem

    def stack2(x, y, p):
        return _pieces(jnp.concatenate([stack(pair(x, p)), stack(pair(y, p))], axis=0), WKV_P_IN)

    ar2 = [stack2(a_t, r_t, p) for p in pairs]
    bk2 = [stack2(b_t, k_t, p) for p in pairs]
    bkh2 = [stack2(b_h, k_h, p) for p in pairs]
    v2 = [stack(pair(v, p)) for p in pairs]

    gram = [_pdot(ar2[p], bk2[p], _NT) for p in pairs]
    lab_lak = [jnp.where(tril_strict, gram[p][:LANES], 0.0) for p in pairs]
    lab = [t[:, :LANES] for t in lab_lak]
    lak = [_pieces(t[:, LANES:], WKV_P_TRI) for t in lab_lak]
    mrbk = [_pieces(jnp.where(tril_incl, gram[p][LANES:], 0.0), WKV_P_TRI) for p in pairs]

    x = [eye + jnp.where(level_mask(1), lab[p], 0.0) for p in pairs]
    for b in (2, 4, 8, 16, 32):
        mask = level_mask(b)
        xp = [_pieces(x[p], WKV_P_TRI) for p in pairs]
        t = [_pdot(xp[p], _pieces(jnp.where(mask, lab[p], 0.0), WKV_P_TRI), _NN) for p in pairs]
        x = [x[p] + _pdot(_pieces(t[p], WKV_P_TRI), xp[p], _NN) for p in pairs]

    s = [s_ref[p] for p in pairs]
    ars = [_pdot(ar2[p], _pieces(s[p], WKV_P_IN), _NT) for p in pairs]
    rhs = [ars[p][:LANES] + _pdot(lak[p], _pieces(v2[p], WKV_P_IN), _NN) for p in pairs]
    u2 = [_pdot(_pieces(x[p], WKV_P_TRI), _pieces(rhs[p], WKV_P_IN), _NN) for p in pairs]
    uv = [_pieces(jnp.concatenate([u2[p], v2[p]], axis=0), WKV_P_IN) for p in pairs]
    y2 = [ars[p][LANES:] + _pdot(mrbk[p], uv[p], _NN) for p in pairs]
    for p in pairs:
        s_ref[p] = s[p] * pair(decay, p) + _pdot(uv[p], bkh2[p], _TN)
    y = jnp.concatenate([t[:C] + t[C:] for t in y2], axis=1)

    mean = head_sum(y) * (1.0 / RWKV_HEAD)
    d = y - mean
    var = head_sum(d * d) * (1.0 / RWKV_HEAD)
    yn = d * lax.rsqrt(var + GN_EPS) * lnw_ref[...] + lnb_ref[...]
    bonus = head_sum(r * kmod * rk_ref[...]) * v
    o_ref[...] = (yn + bonus) * g_ref[...]


def _wkv(r, k, v, a, lw, g, s0_bd, k_k, k_a, r_k, ln_w, ln_b):
    B, T, D = r.shape
    npairs = D // LANES
    xspec = pl.BlockSpec((None, CHUNK, D), lambda b, i: (b, i, 0))
    sspec = pl.BlockSpec((None, npairs, LANES, LANES), lambda b, i: (b, 0, 0, 0))
    pspec = pl.BlockSpec((1, D), lambda b, i: (0, 0))
    return pl.pallas_call(
        _wkv_kernel,
        grid=(B, T // CHUNK),
        in_specs=[xspec] * 6 + [sspec] + [pspec] * 5,
        out_specs=[xspec, sspec],
        out_shape=[jax.ShapeDtypeStruct((B, T, D), F32),
                   jax.ShapeDtypeStruct((B, npairs, LANES, LANES), F32)],
        compiler_params=_cparams(("parallel", "arbitrary")),
        name="wkv",
    )(r, k, v, a, lw, g, s0_bd, k_k, k_a, r_k, ln_w, ln_b)


def _state_to_blockdiag(s):
    B, H, N, _ = s.shape
    s = s.reshape(B, H // 2, 2, N, N)
    z = jnp.zeros_like(s[:, :, 0])
    top = jnp.concatenate([s[:, :, 0], z], axis=-1)
    bot = jnp.concatenate([z, s[:, :, 1]], axis=-1)
    return jnp.concatenate([top, bot], axis=-2)


def _state_from_blockdiag(s):
    B, P, N2, _ = s.shape
    N = N2 // 2
    return jnp.stack([s[:, :, :N, :N], s[:, :, N:, N:]], axis=2).reshape(B, 2 * P, N, N)


def _proj_res_kernel(x_ref, a_ref, g_ref, ng_ref, w_ref, o_ref):
    bb, tt, D = x_ref.shape
    a = a_ref[...].reshape(bb * tt, a_ref.shape[-1]).astype(BF16)
    out = _dot(a, w_ref[...]).reshape(bb, tt, D)
    o_ref[...] = x_ref[...] + g_ref[...] * _rms(out, ng_ref[...])


def _proj_res(x, a, g, ng, w, rows=512):
    B, T, D = x.shape
    bb, tt = _tile(B, T, rows)
    xspec = pl.BlockSpec((bb, tt, D), lambda b, i: (b, i, 0))
    return pl.pallas_call(
        _proj_res_kernel,
        grid=(B // bb, T // tt),
        in_specs=[xspec, pl.BlockSpec((bb, tt, a.shape[-1]), lambda b, i: (b, i, 0)),
                  pl.BlockSpec((bb, 1, D), lambda b, i: (b, 0, 0)),
                  pl.BlockSpec((1, D), lambda b, i: (0, 0)),
                  pl.BlockSpec(w.shape, lambda b, i: (0, 0))],
        out_specs=xspec,
        out_shape=jax.ShapeDtypeStruct((B, T, D), F32),
        compiler_params=_cparams(("parallel", "parallel")),
        name="proj_res",
    )(x, a, g, ng, w)


def _rope_tables(pos):
    half = ROT_DIM // 2
    inv = ROPE_THETA ** (-jnp.arange(half, dtype=F32) * 2.0 / ROT_DIM)
    ang = pos.astype(F32)[:, None] * inv[None, :]
    cos, sin = jnp.cos(ang), jnp.sin(ang)
    T = pos.shape[0]
    ones = jnp.ones((T, HEAD_QK - ROT_DIM), F32)
    zeros = jnp.zeros((T, HEAD_QK - ROT_DIM), F32)
    zh = jnp.zeros((T, half), F32)
    c = jnp.concatenate([cos, cos, ones], axis=-1)
    s1 = jnp.concatenate([-sin, zh, zeros], axis=-1)
    s2 = jnp.concatenate([zh, sin, zeros], axis=-1)
    return tuple(jnp.concatenate([t, t], axis=-1) for t in (c, s1, s2))


def _rope_rows(y, c, s1, s2):
    half = ROT_DIM // 2
    outs = []
    for p in range(y.shape[-1] // LANES):
        xs = y[:, p * LANES:(p + 1) * LANES]
        outs.append(xs * c + pltpu.roll(xs, LANES - half, axis=1) * s1 + pltpu.roll(xs, half, axis=1) * s2)
    return jnp.concatenate(outs, axis=-1)


def _nmm_kv_kernel(x_ref, s_ref, c_ref, ng_ref, w_ref, tc_ref, ts1_ref, ts2_ref,
                   k_ref, v_ref, kb_ref, vb_ref):
    bb, tt, D = x_ref.shape
    h = _rms(x_ref[...], ng_ref[...]) * (1.0 + c_ref[...]) + s_ref[...]
    kv = _dot(h.reshape(bb * tt, D).astype(BF16), w_ref[...])
    nk = k_ref.shape[-1]
    k = _rope_rows(kv[:, :nk], tc_ref[...], ts1_ref[...], ts2_ref[...]).reshape(bb, tt, nk)
    v = kv[:, nk:].reshape(bb, tt, v_ref.shape[-1])
    k_ref[...] = k
    v_ref[...] = v
    kb_ref[...] = k.astype(BF16)
    vb_ref[...] = v.astype(BF16)


def _nmm_q_kernel(x_ref, s_ref, c_ref, ng_ref, w_ref, tc_ref, ts1_ref, ts2_ref, q_ref):
    bb, tt, D = x_ref.shape
    h = _rms(x_ref[...], ng_ref[...]) * (1.0 + c_ref[...]) + s_ref[...]
    q = _dot(h.reshape(bb * tt, D).astype(BF16), w_ref[...])
    q = _rope_rows(q, tc_ref[...], ts1_ref[...], ts2_ref[...]) * (ATTN_SCALE * LOG2E)
    q_ref[...] = q.reshape(bb, tt, q_ref.shape[-1]).astype(BF16)


def _nmm_rope(body, x, s, c, ng, w, pos, out_widths, out_dtypes, rows=512):
    B, T, D = x.shape
    bb, tt = _tile(B, T, rows)
    tabs = _rope_tables(pos)
    if bb > 1:
        tabs = tuple(jnp.tile(t, (bb, 1)) for t in tabs)
        tspec = pl.BlockSpec((bb * tt, LANES), lambda b, i: (0, 0))
    else:
        tspec = pl.BlockSpec((tt, LANES), lambda b, i: (i, 0))
    xspec = pl.BlockSpec((bb, tt, D), lambda b, i: (b, i, 0))
    mspec = pl.BlockSpec((bb, 1, D), lambda b, i: (b, 0, 0))
    return pl.pallas_call(
        body,
        grid=(B // bb, T // tt),
        in_specs=[xspec, mspec, mspec, pl.BlockSpec((1, D), lambda b, i: (0, 0)),
                  pl.BlockSpec(w.shape, lambda b, i: (0, 0)), tspec, tspec, tspec],
        out_specs=[pl.BlockSpec((bb, tt, n), lambda b, i: (b, i, 0)) for n in out_widths],
        out_shape=[jax.ShapeDtypeStruct((B, T, n), dt) for n, dt in zip(out_widths, out_dtypes)],
        compiler_params=_cparams(("parallel", "parallel")),
        name="nmm_rope",
    )(x, s, c, ng, w, *tabs)


def _attn_kernel(*refs, q_off, tk, lam_init, has_past):
    if has_past:
        q_ref, kp_ref, vp_ref, kn_ref, vn_ref, lam_ref, sg_ref, o_ref, m_scr, acc_scr, p_scr = refs
    else:
        q_ref, kn_ref, vn_ref, lam_ref, sg_ref, o_ref, m_scr, acc_scr, p_scr = refs
        kp_ref, vp_ref = kn_ref, vn_ref
    tq = q_ref.shape[0]
    qi = pl.program_id(2)
    n_full = (q_off + qi * tq) // tk
    lane = lax.broadcasted_iota(jnp.int32, (tq, LANES), 1)

    def one_head(ls):
        q = q_ref[:, ls]
        zero = jnp.zeros_like(q)
        qs = jnp.concatenate([jnp.where(lane < HEAD_QK, q, zero), jnp.where(lane < HEAD_QK, zero, q)], axis=0)

        m_scr[...] = jnp.full_like(m_scr, NEG_INF)
        acc_scr[...] = jnp.zeros_like(acc_scr)
        p_scr[...] = jnp.zeros_like(p_scr)

        def with_ones(v):
            return jnp.concatenate([v, jnp.ones_like(v)], axis=1)

        def pending_pv(j):
            start = pl.multiple_of(jnp.maximum(j - 1, 0) * tk, tk)
            return _dot(p_scr[...], with_ones(vp_ref[pl.ds(start, tk), ls]))

        def softmax_step(s, size):
            m_old = m_scr[...]
            m_new = jnp.maximum(m_old, jnp.max(s, axis=-1, keepdims=True))
            m_wide = m_new[:, :size] if size < LANES else jnp.concatenate([m_new] * (size // LANES), axis=1)
            p = jnp.exp2(s - m_wide).astype(BF16)
            alpha = jnp.exp2(m_old - m_new)
            m_scr[...] = m_new
            return p, jnp.concatenate([alpha, alpha], axis=1)

        def full_tile(j, carry):
            kt = kp_ref[pl.ds(pl.multiple_of(j * tk, tk), tk), ls]
            s = lax.dot_general(qs, kt, (((1,), (1,)), ((), ())), preferred_element_type=F32)
            pv = pending_pv(j)
            p, alpha = softmax_step(s, tk)
            acc_scr[...] = alpha * (acc_scr[...] + pv)
            p_scr[...] = p
            return carry

        lax.fori_loop(0, n_full, full_tile, 0)

        start = pl.multiple_of(qi * tq, CHUNK)
        row = lax.broadcasted_iota(jnp.int32, (2 * tq, tq), 0)
        col = lax.broadcasted_iota(jnp.int32, (2 * tq, tq), 1)
        visible = (col // CHUNK) <= ((row % tq) // CHUNK)
        s = lax.dot_general(qs, kn_ref[pl.ds(start, tq), ls], (((1,), (1,)), ((), ())), preferred_element_type=F32)
        pv = pending_pv(n_full)
        p, alpha = softmax_step(jnp.where(visible, s, NEG_INF), tq)
        acc = alpha * (acc_scr[...] + pv) + _dot(p, with_ones(vn_ref[pl.ds(start, tq), ls]))

        lp = lam_ref[...]
        lam = (jnp.exp(jnp.sum(lp[0:1] * lp[1:2], axis=-1, keepdims=True))
               - jnp.exp(jnp.sum(lp[2:3] * lp[3:4], axis=-1, keepdims=True)) + lam_init)
        o = acc[:, :LANES] / acc[:, LANES:]
        o = o[:tq] - lam * o[tq:]
        o = o * lax.rsqrt(jnp.mean(o * o, axis=-1, keepdims=True) + NORM_EPS)
        o_ref[:, ls] = o * sg_ref[...] * (1.0 - lam_init)

    for hh in range(q_ref.shape[1] // LANES):
        one_head(slice(hh * LANES, (hh + 1) * LANES))


def _attn(q, k_new, v_new, k_past, v_past, lam_p, subln_g, lam_init, tile=512):
    B, T, W = q.shape
    H = W // LANES
    tq = min(T, tile)
    has_past = k_past is not None
    q_off = k_past.shape[1] if has_past else 0
    assert T % tq == 0 and q_off % tile == 0 and (T == tq if has_past else tq == tile)
    heads_per_step = 1
    wl = heads_per_step * LANES
    qspec = pl.BlockSpec((None, tq, wl), lambda b, h, i: (b, i, h))

    def kspec(a):
        return pl.BlockSpec((None, a.shape[1], wl), lambda b, h, i: (b, 0, h))

    kv = [k_past, v_past, k_new, v_new] if has_past else [k_new, v_new]
    return pl.pallas_call(
        functools.partial(_attn_kernel, q_off=q_off, tk=tile, lam_init=lam_init, has_past=has_past),
        grid=(B, H // heads_per_step, T // tq),
        in_specs=[qspec] + [kspec(a) for a in kv] +
                 [pl.BlockSpec(lam_p.shape, lambda b, h, i: (0, 0)), pl.BlockSpec((1, LANES), lambda b, h, i: (0, 0))],
        out_specs=qspec,
        out_shape=jax.ShapeDtypeStruct((B, T, W), F32),
        scratch_shapes=[pltpu.VMEM((2 * tq, LANES), F32), pltpu.VMEM((2 * tq, 2 * LANES), F32),
                        pltpu.VMEM((2 * tq, tile), BF16)],
        compiler_params=_cparams(("parallel", "parallel", "arbitrary")),
        name="attn",
    )(q, *kv, lam_p, subln_g)


def _trunk(x, mods, kv_mod, past_len, wkv0, shift0, k_past, v_past, P):
    B, T, D = x.shape
    pos = past_len + jnp.arange(T)

    def chunks(m, n):
        return [m[:, None, j * D:(j + 1) * D] for j in range(n)]

    row = lambda a: a.reshape(1, -1)
    for l in range(2):
        s1, c1, g1, s2, c2, g2, s3, c3, g3 = chunks(mods[l], 9)
        ng = P['norm_g'][l]
        if l == 1:
            kv_shift, kv_scale = chunks(kv_mod, 2)
            k_new, v_new, kb, vb = _nmm_rope(_nmm_kv_kernel, x, kv_shift, kv_scale, row(P['kv_norm_g']),
                                             P['kv_w'], pos, (D, D, D, D), (F32, F32, BF16, BF16))
            kpb = vpb = None
            if k_past is not None:
                kpb = k_past.reshape(B, past_len, D).astype(BF16)
                vpb = v_past.reshape(B, past_len, D).astype(BF16)
        x = _ffn(x, s1, c1, g1, row(ng[0]), row(ng[1]), P['ffn_w_in'], P['ffn_w_out'], l, 0)
        if l == 0:
            r, k, v, a, lw, g, shift = _rwkv_pre(
                x, s2, c2, row(ng[2]), shift0, P['rwkv_mu'], P['rwkv_w_rkv'], row(P['rwkv_w0']),
                P['rwkv_w1'], P['rwkv_w2'], row(P['rwkv_a0']), P['rwkv_a1'], P['rwkv_a2'],
                P['rwkv_g1'], P['rwkv_g2'])
            o, s_bd = _wkv(r, k, v, a, lw, g, _state_to_blockdiag(wkv0), row(P['rwkv_k_k']),
                           row(P['rwkv_k_a']), row(P['rwkv_r_k']), row(P['rwkv_ln_w']), row(P['rwkv_ln_b']))
            wkv = _state_from_blockdiag(s_bd)
            x = _proj_res(x, o, g2, row(ng[3]), P['rwkv_w_o'])
        else:
            (q,) = _nmm_rope(_nmm_q_kernel, x, s2, c2, row(ng[2]), P['diff_w_q'], pos, (D,), (BF16,))
            lam_init = 0.8 - 0.6 * math.exp(-0.3 * l)
            o = _attn(q, kb, vb, kpb, vpb, P['diff_lambda'], row(P['diff_subln_g']), lam_init)
            x = _proj_res(x, o, g2, row(ng[3]), P['diff_w_o'])
        x = _ffn(x, s3, c3, g3, row(ng[4]), row(ng[5]), P['ffn_w_in'], P['ffn_w_out'], l, 1)
    return (x, wkv[None], shift[None],
            k_new.reshape(B, T, N_HEADS_B, 2, HEAD_QK), v_new.reshape(B, T, N_HEADS_B, HEAD_V))


def kernel(x_prompt, x_sample, c_prompt, c_sample, state_wkv, state_shift, cache_k, cache_v, ada_w, ada_b, norm_g, ffn_w_in, ffn_w_out, rwkv_mu, rwkv_w_rkv, rwkv_w0, rwkv_w1, rwkv_w2, rwkv_a0, rwkv_a1, rwkv_a2, rwkv_g1, rwkv_g2, rwkv_k_k, rwkv_k_a, rwkv_r_k, rwkv_ln_w, rwkv_ln_b, rwkv_w_o, kv_ada_w, kv_ada_b, kv_norm_g, kv_w, diff_w_q, diff_lambda, diff_subln_g, diff_w_o):
    Bp, _, D = x_prompt.shape
    Bs = x_sample.shape[0]
    H = D // RWKV_HEAD
    bf = lambda w: w.astype(BF16)
    P = dict(norm_g=norm_g, ffn_w_in=bf(ffn_w_in), ffn_w_out=bf(ffn_w_out),
             rwkv_mu=rwkv_mu[0], rwkv_w_rkv=bf(rwkv_w_rkv[0]), rwkv_w0=rwkv_w0[0], rwkv_w1=bf(rwkv_w1[0]),
             rwkv_w2=bf(rwkv_w2[0]), rwkv_a0=rwkv_a0[0], rwkv_a1=bf(rwkv_a1[0]), rwkv_a2=bf(rwkv_a2[0]),
             rwkv_g1=bf(rwkv_g1[0]), rwkv_g2=bf(rwkv_g2[0]), rwkv_k_k=rwkv_k_k[0], rwkv_k_a=rwkv_k_a[0],
             rwkv_r_k=rwkv_r_k[0], rwkv_ln_w=rwkv_ln_w[0], rwkv_ln_b=rwkv_ln_b[0], rwkv_w_o=bf(rwkv_w_o[0]),
             kv_norm_g=kv_norm_g, kv_w=bf(kv_w), diff_w_q=bf(diff_w_q[0]), diff_lambda=diff_lambda[0],
             diff_subln_g=diff_subln_g[0], diff_w_o=bf(diff_w_o[0]))

    c_all = jnp.concatenate([c_prompt, c_sample], axis=0)
    mods = _ada(c_all, ada_w, ada_b[:, None, :])
    kv_mod = _ada(c_all, kv_ada_w[None], kv_ada_b[None, None, :])[0]

    wkv0 = jnp.zeros((Bp, H, RWKV_HEAD, RWKV_HEAD), F32)
    shift0 = jnp.zeros((Bp, 1, D), F32)
    y_p, wkv_p, shift_p, k_p, v_p = _trunk(
        x_prompt, [mods[0, :Bp], mods[1, :Bp]], kv_mod[:Bp], 0, wkv0, shift0, None, None, P)
    y_s, wkv_s, shift_s, k_s, v_s = _trunk(
        x_sample, [mods[0, Bp:], mods[1, Bp:]], kv_mod[Bp:], cache_k.shape[1],
        state_wkv[0], state_shift[0], cache_k, cache_v, P)
    return (y_p, y_s, wkv_p, shift_p, k_p, v_p, wkv_s, shift_s, k_s, v_s)
```

```python
import functools
import math

import jax
import jax.numpy as jnp
from jax import lax
from jax.experimental import pallas as pl
from jax.experimental.pallas import tpu as pltpu

F32 = jnp.float32
BF16 = jnp.bfloat16
HI = lax.Precision.HIGHEST

CHUNK = 64
RWKV_HEAD = 64
GN_EPS = 64e-5
N_HEADS_B = 8
HEAD_QK = 64
HEAD_V = 128
ROT_DIM = HEAD_QK // 4
ROPE_THETA = 500000.0
ATTN_SCALE = HEAD_QK ** -0.5
LOG2E = math.log2(math.e)
NORM_EPS = 1e-6
NEG_INF = -1e30

LANES = 128
VMEM_LIMIT = 56 * 1024 * 1024


def _cparams(sem):
    return pltpu.CompilerParams(dimension_semantics=sem, vmem_limit_bytes=VMEM_LIMIT)


def _tile(B, T, rows):
    if T >= rows:
        assert T % rows == 0
        return 1, rows
    bb = min(B, max(1, rows // T))
    while B % bb:
        bb -= 1
    return bb, T


def _rms(x, g):
    return x * lax.rsqrt(jnp.mean(x * x, axis=-1, keepdims=True) + NORM_EPS) * g


def _sigmoid(x):
    return 1.0 / (1.0 + jnp.exp(-x))


def _dot(a, b):
    return jnp.dot(a, b, preferred_element_type=F32)


def _dot_hi(a, b):
    return jnp.dot(a, b, preferred_element_type=F32, precision=HI)


def _dot_nt_hi(a, b):
    return lax.dot_general(a, b, (((1,), (1,)), ((), ())), preferred_element_type=F32, precision=HI)


def _dot_tn_hi(a, b):
    return lax.dot_general(a, b, (((0,), (0,)), ((), ())), preferred_element_type=F32, precision=HI)


def _ada_kernel(c_ref, w_ref, b_ref, o_ref):
    c = c_ref[...]
    sc = (c * _sigmoid(c)).astype(BF16)
    o_ref[...] = _dot(sc, w_ref[...].astype(BF16)) + b_ref[...]


def _ada(c, w, b, tn=1024):
    L, D, N = w.shape
    M = c.shape[0]
    return pl.pallas_call(
        _ada_kernel,
        grid=(L, N // tn),
        in_specs=[pl.BlockSpec((M, D), lambda l, j: (0, 0)),
                  pl.BlockSpec((None, D, tn), lambda l, j: (l, 0, j)),
                  pl.BlockSpec((None, 1, tn), lambda l, j: (l, 0, j))],
        out_specs=pl.BlockSpec((None, M, tn), lambda l, j: (l, 0, j)),
        out_shape=jax.ShapeDtypeStruct((L, M, N), F32),
        compiler_params=_cparams(("parallel", "parallel")),
        name="ada",
    )(c, w, b)


def _ffn_kernel(*refs, tf, has_mixer):
    if has_mixer:
        a_ref, mg_ref, mng_ref, mw_ref = refs[:4]
        refs = refs[4:]
    x_ref, s_ref, c_ref, g_ref, ngpre_ref, ngpost_ref, wg_ref, wu_ref, wo_ref, o_ref = refs
    bb, tt, D = x_ref.shape
    x = x_ref[...]
    if has_mixer:
        a = a_ref[...].reshape(bb * tt, a_ref.shape[-1]).astype(BF16)
        x = x + mg_ref[...] * _rms(_dot(a, mw_ref[...]).reshape(bb, tt, D), mng_ref[...])
    h = (_rms(x, ngpre_ref[...]) * (1.0 + c_ref[...]) + s_ref[...]).reshape(bb * tt, D).astype(BF16)
    acc = None
    for j in range(wo_ref.shape[0] // tf):
        cols = slice(j * tf, (j + 1) * tf)
        gate = _dot(h, wg_ref[:, cols])
        up = _dot(h, wu_ref[:, cols])
        act = ((gate * _sigmoid(gate)) * up).astype(BF16)
        part = _dot(act, wo_ref[cols, :])
        acc = part if acc is None else acc + part
    o_ref[...] = x + 0.5 * g_ref[...] * _rms(acc.reshape(bb, tt, D), ngpost_ref[...])


def _ffn(x, s, c, g, ng_pre, ng_post, w_in, w_out, l, m, mixer=None, rows=512, tf=256):
    B, T, D = x.shape
    Fh = w_out.shape[2]
    bb, tt = _tile(B, T, rows)
    xspec = pl.BlockSpec((bb, tt, D), lambda b, i: (b, i, 0))
    mspec = pl.BlockSpec((bb, 1, D), lambda b, i: (b, 0, 0))
    gspec = pl.BlockSpec((1, D), lambda b, i: (0, 0))
    once = pl.Buffered(1)
    in_specs = [xspec, mspec, mspec, mspec, gspec, gspec,
                pl.BlockSpec((None, None, D, Fh), lambda b, i: (l, m, 0, 0), pipeline_mode=once),
                pl.BlockSpec((None, None, D, Fh), lambda b, i: (l, m, 0, 1), pipeline_mode=once),
                pl.BlockSpec((None, None, Fh, D), lambda b, i: (l, m, 0, 0), pipeline_mode=once)]
    args = (x, s, c, g, ng_pre, ng_post, w_in, w_in, w_out)
    if mixer is not None:
        a, mg, mng, mw = mixer
        in_specs = [pl.BlockSpec((bb, tt, a.shape[-1]), lambda b, i: (b, i, 0)), mspec, gspec,
                    pl.BlockSpec(mw.shape, lambda b, i: (0, 0), pipeline_mode=once)] + in_specs
        args = (a, mg, mng, mw) + args
    return pl.pallas_call(
        functools.partial(_ffn_kernel, tf=tf, has_mixer=mixer is not None),
        grid=(B // bb, T // tt),
        in_specs=in_specs,
        out_specs=xspec,
        out_shape=jax.ShapeDtypeStruct((B, T, D), F32),
        compiler_params=_cparams(("parallel", "parallel")),
        name="ffn",
    )(*args)


def _rwkv_pre_kernel(x_ref, s_ref, c_ref, ng_ref, shift_ref, mu_ref, wrkv_ref, w0_ref, w1_ref, w2_ref,
                     a0_ref, a1_ref, a2_ref, g1_ref, g2_ref,
                     r_ref, k_ref, v_ref, a_ref, lw_ref, g_ref, shift_out_ref, prev_scr):
    bb, tt, D = x_ref.shape

    @pl.when(pl.program_id(1) == 0)
    def _():
        prev_scr[...] = shift_ref[...]

    hm = _rms(x_ref[...], ng_ref[...]) * (1.0 + c_ref[...]) + s_ref[...]
    row = lax.broadcasted_iota(jnp.int32, hm.shape, 1)
    xprev = jnp.where(row == 0, prev_scr[...], pltpu.roll(hm, 1, axis=1))
    last = hm[:, tt - 1:tt, :]
    prev_scr[...] = last
    shift_out_ref[...] = last
    dx = xprev - hm

    def mix(j):
        return (hm + dx * mu_ref[j:j + 1, :]).reshape(bb * tt, D).astype(BF16)

    def out(ref, val):
        ref[...] = val.reshape(bb, tt, D)

    out(r_ref, _dot(mix(0), wrkv_ref[0]))
    out(k_ref, _dot(mix(2), wrkv_ref[1]))
    out(v_ref, _dot(mix(3), wrkv_ref[2]))
    z = w0_ref[...] + _dot(jnp.tanh(_dot(mix(1), w1_ref[...])).astype(BF16), w2_ref[...])
    nz = -z
    softplus = jnp.maximum(nz, 0.0) + jnp.log1p(jnp.exp(-jnp.abs(nz)))
    out(lw_ref, -jnp.exp(-softplus - 0.5))
    out(a_ref, _sigmoid(a0_ref[...] + _dot(_dot(mix(4), a1_ref[...]).astype(BF16), a2_ref[...])))
    out(g_ref, _dot(_sigmoid(_dot(mix(5), g1_ref[...])).astype(BF16), g2_ref[...]))


def _rwkv_pre(x, s, c, ng, shift_in, mu, wrkv, w0, w1, w2, a0, a1, a2, g1, g2, rows=256):
    B, T, D = x.shape
    bb, tt = _tile(B, T, rows)
    xspec = pl.BlockSpec((bb, tt, D), lambda b, i: (b, i, 0))
    mspec = pl.BlockSpec((bb, 1, D), lambda b, i: (b, 0, 0))

    def full(a):
        return pl.BlockSpec(a.shape, lambda b, i: (0,) * a.ndim)

    big = jax.ShapeDtypeStruct((B, T, D), F32)
    return pl.pallas_call(
        _rwkv_pre_kernel,
        grid=(B // bb, T // tt),
        in_specs=[xspec, mspec, mspec, full(ng), mspec, full(mu), full(wrkv), full(w0), full(w1), full(w2),
                  full(a0), full(a1), full(a2), full(g1), full(g2)],
        out_specs=[xspec] * 6 + [mspec],
        out_shape=[big] * 6 + [jax.ShapeDtypeStruct((B, 1, D), F32)],
        scratch_shapes=[pltpu.VMEM((bb, 1, D), F32)],
        compiler_params=_cparams(("parallel", "arbitrary")),
        name="rwkv_pre",
    )(x, s, c, ng, shift_in, mu, wrkv, w0, w1, w2, a0, a1, a2, g1, g2)


def _pieces(x, n):
    out = []
    for _ in range(n):
        p = x.astype(BF16)
        out.append(p)
        x = x - p.astype(F32)
    return out


def _pdot(a, b, dims):
    order = max(len(a), len(b))
    acc = None
    for i, ai in enumerate(a):
        for j, bj in enumerate(b):
            if i + j < order:
                t = lax.dot_general(ai, bj, (dims, ((), ())), preferred_element_type=F32)
                acc = t if acc is None else acc + t
    return acc


_NN = ((1,), (0,))
_NT = ((1,), (1,))
_TN = ((0,), (0,))

WKV_P_IN = 1
WKV_P_TRI = 1


def _wkv_kernel(r_ref, k_ref, v_ref, a_ref, lw_ref, g_ref, s0_ref, kk_ref, ka_ref, rk_ref, lnw_ref, lnb_ref,
                o_ref, s_ref):
    C = CHUNK
    L2 = 2 * LANES
    npairs = s_ref.shape[0]

    @pl.when(pl.program_id(1) == 0)
    def _():
        s_ref[...] = s0_ref[...]

    ri = lax.broadcasted_iota(jnp.int32, (LANES, L2), 0)
    ci = lax.broadcasted_iota(jnp.int32, (LANES, L2), 1) % LANES
    same_head = (ri // C) == (ci // C)
    tril_incl = same_head & ((ci % C) <= (ri % C))
    tril_strict = same_head & ((ci % C) < (ri % C))
    ri1 = lax.broadcasted_iota(jnp.int32, (LANES, LANES), 0)
    ci1 = lax.broadcasted_iota(jnp.int32, (LANES, LANES), 1)
    head_ones = [((ri1 // C) == (ci1 // C)).astype(F32).astype(BF16)]
    eye = (ri1 == ci1).astype(F32)
    r64 = lax.broadcasted_iota(jnp.int32, (C, C), 0)
    c64 = lax.broadcasted_iota(jnp.int32, (C, C), 1)
    cumsum_mat = [(c64 <= r64).astype(F32).astype(BF16)]
    lane = lax.broadcasted_iota(jnp.int32, (C, LANES), 1)
    first_head = lane < RWKV_HEAD

    def stack(x):
        return jnp.concatenate([jnp.where(first_head, x, 0.0), jnp.where(first_head, 0.0, x)], axis=0)

    def level_mask(b):
        return ((ri1 // (2 * b)) == (ci1 // (2 * b))) & ((ri1 % (2 * b)) >= b) & ((ci1 % (2 * b)) < b)

    pairs = range(npairs)

    def pair(x, p):
        return x[:, p * LANES:(p + 1) * LANES]

    def head_sum(x):
        xs = jnp.concatenate([pair(x, p) for p in pairs], axis=0)
        ys = _pdot(_pieces(xs, 2), head_ones, _NN)
        return jnp.concatenate([ys[p * C:(p + 1) * C] for p in pairs], axis=1)

    def chunk(ci, carry):
        rows = pl.ds(pl.multiple_of(ci * C, C), C)
        r, k, v, a, lw = r_ref[rows, :], k_ref[rows, :], v_ref[rows, :], a_ref[rows, :], lw_ref[rows, :]
        kk = k * kk_ref[...]
        kk = kk / jnp.maximum(jnp.sqrt(head_sum(kk * kk)), 1e-12)
        kmod = k * (1.0 + (a - 1.0) * ka_ref[...])
        bvec = kk * a
        cum = _pdot(cumsum_mat, _pieces(lw, 3), _NN)
        total = cum[C - 1:C, :]
        inv = jnp.exp(-cum)
        rem = jnp.exp(total - cum)
        decay = jnp.exp(total)
        a_t, r_t = -kk * jnp.exp(cum - lw), r * jnp.exp(cum)
        b_t, k_t = bvec * inv, kmod * inv
        b_h, k_h = bvec * rem, kmod * rem

        def stack2(x, y, p):
            return _pieces(jnp.concatenate([stack(pair(x, p)), stack(pair(y, p))], axis=0), WKV_P_IN)

        ar2 = [stack2(a_t, r_t, p) for p in pairs]
        bk2 = [stack2(b_t, k_t, p) for p in pairs]
        bkh2 = [stack2(b_h, k_h, p) for p in pairs]
        v2 = [stack(pair(v, p)) for p in pairs]

        gram = [_pdot(ar2[p], bk2[p], _NT) for p in pairs]
        lab_lak = [jnp.where(tril_strict, gram[p][:LANES], 0.0) for p in pairs]
        lab = [t[:, :LANES] for t in lab_lak]
        lak = [_pieces(t[:, LANES:], WKV_P_TRI) for t in lab_lak]
        mrbk = [_pieces(jnp.where(tril_incl, gram[p][LANES:], 0.0), WKV_P_TRI) for p in pairs]

        x = [eye + jnp.where(level_mask(1), lab[p], 0.0) for p in pairs]
        for b in (2, 4, 8, 16, 32):
            mask = level_mask(b)
            xp = [_pieces(x[p], WKV_P_TRI) for p in pairs]
            t = [_pdot(xp[p], _pieces(jnp.where(mask, lab[p], 0.0), WKV_P_TRI), _NN) for p in pairs]
            x = [x[p] + _pdot(_pieces(t[p], WKV_P_TRI), xp[p], _NN) for p in pairs]

        s = [s_ref[p] for p in pairs]
        ars = [_pdot(ar2[p], _pieces(s[p], WKV_P_IN), _NT) for p in pairs]
        rhs = [ars[p][:LANES] + _pdot(lak[p], _pieces(v2[p], WKV_P_IN), _NN) for p in pairs]
        u2 = [_pdot(_pieces(x[p], WKV_P_TRI), _pieces(rhs[p], WKV_P_IN), _NN) for p in pairs]
        uv = [_pieces(jnp.concatenate([u2[p], v2[p]], axis=0), WKV_P_IN) for p in pairs]
        y2 = [ars[p][LANES:] + _pdot(mrbk[p], uv[p], _NN) for p in pairs]
        for p in pairs:
            s_ref[p] = s[p] * pair(decay, p) + _pdot(uv[p], bkh2[p], _TN)
        y = jnp.concatenate([t[:C] + t[C:] for t in y2], axis=1)

        mean = head_sum(y) * (1.0 / RWKV_HEAD)
        d = y - mean
        var = head_sum(d * d) * (1.0 / RWKV_HEAD)
        yn = d * lax.rsqrt(var + GN_EPS) * lnw_ref[...] + lnb_ref[...]
        bonus = head_sum(r * kmod * rk_ref[...]) * v
        o_ref[rows, :] = (yn + bonus) * g_ref[rows, :]
        return carry

    lax.fori_loop(0, r_ref.shape[0] // C, chunk, 0)


def _wkv(r, k, v, a, lw, g, s0_bd, k_k, k_a, r_k, ln_w, ln_b):
    B, T, D = r.shape
    npairs = D // LANES
    rows = min(T, 4 * CHUNK)
    xspec = pl.BlockSpec((None, rows, D), lambda b, i: (b, i, 0))
    sspec = pl.BlockSpec((None, npairs, LANES, LANES), lambda b, i: (b, 0, 0, 0))
    pspec = pl.BlockSpec((1, D), lambda b, i: (0, 0))
    return pl.pallas_call(
        _wkv_kernel,
        grid=(B, T // rows),
        in_specs=[xspec] * 6 + [sspec] + [pspec] * 5,
        out_specs=[xspec, sspec],
        out_shape=[jax.ShapeDtypeStruct((B, T, D), F32),
                   jax.ShapeDtypeStruct((B, npairs, LANES, LANES), F32)],
        compiler_params=_cparams(("parallel", "arbitrary")),
        name="wkv",
    )(r, k, v, a, lw, g, s0_bd, k_k, k_a, r_k, ln_w, ln_b)


def _state_to_blockdiag(s):
    B, H, N, _ = s.shape
    s = s.reshape(B, H // 2, 2, N, N)
    z = jnp.zeros_like(s[:, :, 0])
    top = jnp.concatenate([s[:, :, 0], z], axis=-1)
    bot = jnp.concatenate([z, s[:, :, 1]], axis=-1)
    return jnp.concatenate([top, bot], axis=-2)


def _state_from_blockdiag(s):
    B, P, N2, _ = s.shape
    N = N2 // 2
    return jnp.stack([s[:, :, :N, :N], s[:, :, N:, N:]], axis=2).reshape(B, 2 * P, N, N)


def _rope_tables(pos):
    half = ROT_DIM // 2
    inv = ROPE_THETA ** (-jnp.arange(half, dtype=F32) * 2.0 / ROT_DIM)
    ang = pos.astype(F32)[:, None] * inv[None, :]
    cos, sin = jnp.cos(ang), jnp.sin(ang)
    T = pos.shape[0]
    ones = jnp.ones((T, HEAD_QK - ROT_DIM), F32)
    zeros = jnp.zeros((T, HEAD_QK - ROT_DIM), F32)
    zh = jnp.zeros((T, half), F32)
    c = jnp.concatenate([cos, cos, ones], axis=-1)
    s1 = jnp.concatenate([-sin, zh, zeros], axis=-1)
    s2 = jnp.concatenate([zh, sin, zeros], axis=-1)
    return tuple(jnp.concatenate([t, t], axis=-1) for t in (c, s1, s2))


def _rope_rows(y, c, s1, s2):
    half = ROT_DIM // 2
    outs = []
    for p in range(y.shape[-1] // LANES):
        xs = y[:, p * LANES:(p + 1) * LANES]
        outs.append(xs * c + pltpu.roll(xs, LANES - half, axis=1) * s1 + pltpu.roll(xs, half, axis=1) * s2)
    return jnp.concatenate(outs, axis=-1)


def _nmm_kv_kernel(x_ref, s_ref, c_ref, ng_ref, w_ref, tc_ref, ts1_ref, ts2_ref,
                   k_ref, v_ref, kb_ref, vb_ref):
    bb, tt, D = x_ref.shape
    h = _rms(x_ref[...], ng_ref[...]) * (1.0 + c_ref[...]) + s_ref[...]
    kv = _dot(h.reshape(bb * tt, D).astype(BF16), w_ref[...])
    nk = k_ref.shape[-1]
    k = _rope_rows(kv[:, :nk], tc_ref[...], ts1_ref[...], ts2_ref[...]).reshape(bb, tt, nk)
    v = kv[:, nk:].reshape(bb, tt, v_ref.shape[-1])
    k_ref[...] = k
    v_ref[...] = v
    kb_ref[...] = k.astype(BF16)
    vb_ref[...] = v.astype(BF16)


def _nmm_q_kernel(x_ref, s_ref, c_ref, ng_ref, w_ref, tc_ref, ts1_ref, ts2_ref, q_ref):
    bb, tt, D = x_ref.shape
    h = _rms(x_ref[...], ng_ref[...]) * (1.0 + c_ref[...]) + s_ref[...]
    q = _dot(h.reshape(bb * tt, D).astype(BF16), w_ref[...])
    q = _rope_rows(q, tc_ref[...], ts1_ref[...], ts2_ref[...]) * (ATTN_SCALE * LOG2E)
    q_ref[...] = q.reshape(bb, tt, q_ref.shape[-1]).astype(BF16)


def _nmm_rope(body, x, s, c, ng, w, pos, out_widths, out_dtypes, rows=512):
    B, T, D = x.shape
    bb, tt = _tile(B, T, rows)
    tabs = _rope_tables(pos)
    if bb > 1:
        tabs = tuple(jnp.tile(t, (bb, 1)) for t in tabs)
        tspec = pl.BlockSpec((bb * tt, LANES), lambda b, i: (0, 0))
    else:
        tspec = pl.BlockSpec((tt, LANES), lambda b, i: (i, 0))
    xspec = pl.BlockSpec((bb, tt, D), lambda b, i: (b, i, 0))
    mspec = pl.BlockSpec((bb, 1, D), lambda b, i: (b, 0, 0))
    return pl.pallas_call(
        body,
        grid=(B // bb, T // tt),
        in_specs=[xspec, mspec, mspec, pl.BlockSpec((1, D), lambda b, i: (0, 0)),
                  pl.BlockSpec(w.shape, lambda b, i: (0, 0)), tspec, tspec, tspec],
        out_specs=[pl.BlockSpec((bb, tt, n), lambda b, i: (b, i, 0)) for n in out_widths],
        out_shape=[jax.ShapeDtypeStruct((B, T, n), dt) for n, dt in zip(out_widths, out_dtypes)],
        compiler_params=_cparams(("parallel", "parallel")),
        name="nmm_rope",
    )(x, s, c, ng, w, *tabs)


def _attn_kernel(*refs, q_off, tk, lam_init, has_past):
    if has_past:
        q_ref, kp_ref, vp_ref, kn_ref, vn_ref, lam_ref, sg_ref, o_ref, m_scr, acc_scr, p_scr = refs
    else:
        q_ref, kn_ref, vn_ref, lam_ref, sg_ref, o_ref, m_scr, acc_scr, p_scr = refs
        kp_ref, vp_ref = kn_ref, vn_ref
    tq = q_ref.shape[0]
    qi = pl.program_id(2)
    n_full = (q_off + qi * tq) // tk
    lane = lax.broadcasted_iota(jnp.int32, (tq, LANES), 1)
    q = q_ref[...]
    zero = jnp.zeros_like(q)
    qs = jnp.concatenate([jnp.where(lane < HEAD_QK, q, zero), jnp.where(lane < HEAD_QK, zero, q)], axis=0)
    row = lax.broadcasted_iota(jnp.int32, (2 * tq, tq), 0)
    col = lax.broadcasted_iota(jnp.int32, (2 * tq, tq), 1)
    visible = (col // CHUNK) <= ((row % tq) // CHUNK)
    own = pl.multiple_of(qi * tq, CHUNK)

    m_scr[...] = jnp.full_like(m_scr, NEG_INF)
    acc_scr[...] = jnp.zeros_like(acc_scr)

    def with_ones(v):
        return jnp.concatenate([v, jnp.ones_like(v)], axis=1)

    def scores(k_ref, start, size):
        return lax.dot_general(qs, k_ref[pl.ds(start, size), :], (((1,), (1,)), ((), ())),
                               preferred_element_type=F32)

    def softmax_step(s, size):
        m_old = m_scr[...]
        m_new = jnp.maximum(m_old, jnp.max(s, axis=-1, keepdims=True))
        m_wide = m_new[:, :size] if size < LANES else jnp.concatenate([m_new] * (size // LANES), axis=1)
        p = jnp.exp2(s - m_wide).astype(BF16)
        alpha = jnp.exp2(m_old - m_new)
        m_scr[...] = m_new
        return p, jnp.concatenate([alpha, alpha], axis=1)

    def finish(acc):
        lp = lam_ref[...]
        lam = (jnp.exp(jnp.sum(lp[0:1] * lp[1:2], axis=-1, keepdims=True))
               - jnp.exp(jnp.sum(lp[2:3] * lp[3:4], axis=-1, keepdims=True)) + lam_init)
        o = acc[:, :LANES] / acc[:, LANES:]
        o = o[:tq] - lam * o[tq:]
        o = o * lax.rsqrt(jnp.mean(o * o, axis=-1, keepdims=True) + NORM_EPS)
        o_ref[...] = o * sg_ref[...] * (1.0 - lam_init)

    p_scr[...] = jnp.zeros_like(p_scr)

    def pending_pv(j):
        start = pl.multiple_of(jnp.maximum(j - 1, 0) * tk, tk)
        return _dot(p_scr[...], with_ones(vp_ref[pl.ds(start, tk), :]))

    def full_tile(j, carry):
        s = scores(kp_ref, pl.multiple_of(j * tk, tk), tk)
        pv = pending_pv(j)
        p, alpha = softmax_step(s, tk)
        acc_scr[...] = alpha * (acc_scr[...] + pv)
        p_scr[...] = p
        return carry

    lax.fori_loop(0, n_full, full_tile, 0)
    pv = pending_pv(n_full)
    p, alpha = softmax_step(jnp.where(visible, scores(kn_ref, own, tq), NEG_INF), tq)
    finish(alpha * (acc_scr[...] + pv) + _dot(p, with_ones(vn_ref[pl.ds(own, tq), :])))


def _attn(q, k_new, v_new, k_past, v_past, lam_p, subln_g, lam_init, tile=512):
    B, T, W = q.shape
    H = W // LANES
    tq = min(T, tile)
    has_past = k_past is not None
    q_off = k_past.shape[1] if has_past else 0
    assert T % tq == 0 and q_off % tile == 0 and (T == tq if has_past else tq == tile)
    qspec = pl.BlockSpec((None, tq, LANES), lambda b, h, i: (b, i, h))

    def kspec(a):
        return pl.BlockSpec((None, a.shape[1], LANES), lambda b, h, i: (b, 0, h))

    kv = [k_past, v_past, k_new, v_new] if has_past else [k_new, v_new]
    scratch = [pltpu.VMEM((2 * tq, LANES), F32), pltpu.VMEM((2 * tq, 2 * LANES), F32),
               pltpu.VMEM((2 * tq, tile), BF16)]
    return pl.pallas_call(
        functools.partial(_attn_kernel, q_off=q_off, tk=tile, lam_init=lam_init, has_past=has_past),
        grid=(B, H, T // tq),
        in_specs=[qspec] + [kspec(a) for a in kv] +
                 [pl.BlockSpec(lam_p.shape, lambda b, h, i: (0, 0)), pl.BlockSpec((1, LANES), lambda b, h, i: (0, 0))],
        out_specs=qspec,
        out_shape=jax.ShapeDtypeStruct((B, T, W), F32),
        scratch_shapes=scratch,
        compiler_params=_cparams(("parallel", "parallel", "arbitrary")),
        name="attn",
    )(q, *kv, lam_p, subln_g)


def _trunk(x, mods, kv_mod, past_len, wkv0, shift0, k_past, v_past, P):
    B, T, D = x.shape
    pos = past_len + jnp.arange(T)

    def chunks(m, n):
        return [m[:, None, j * D:(j + 1) * D] for j in range(n)]

    row = lambda a: a.reshape(1, -1)
    for l in range(2):
        s1, c1, g1, s2, c2, g2, s3, c3, g3 = chunks(mods[l], 9)
        ng = P['norm_g'][l]
        if l == 1:
            kv_shift, kv_scale = chunks(kv_mod, 2)
            k_new, v_new, kb, vb = _nmm_rope(_nmm_kv_kernel, x, kv_shift, kv_scale, row(P['kv_norm_g']),
                                             P['kv_w'], pos, (D, D, D, D), (F32, F32, BF16, BF16))
            kpb = vpb = None
            if k_past is not None:
                kpb = k_past.reshape(B, past_len, D).astype(BF16)
                vpb = v_past.reshape(B, past_len, D).astype(BF16)
        x = _ffn(x, s1, c1, g1, row(ng[0]), row(ng[1]), P['ffn_w_in'], P['ffn_w_out'], l, 0)
        if l == 0:
            r, k, v, a, lw, g, shift = _rwkv_pre(
                x, s2, c2, row(ng[2]), shift0, P['rwkv_mu'], P['rwkv_w_rkv'], row(P['rwkv_w0']),
                P['rwkv_w1'], P['rwkv_w2'], row(P['rwkv_a0']), P['rwkv_a1'], P['rwkv_a2'],
                P['rwkv_g1'], P['rwkv_g2'])
            o, s_bd = _wkv(r, k, v, a, lw, g, _state_to_blockdiag(wkv0), row(P['rwkv_k_k']),
                           row(P['rwkv_k_a']), row(P['rwkv_r_k']), row(P['rwkv_ln_w']), row(P['rwkv_ln_b']))
            wkv = _state_from_blockdiag(s_bd)
            w_o = P['rwkv_w_o']
        else:
            (q,) = _nmm_rope(_nmm_q_kernel, x, s2, c2, row(ng[2]), P['diff_w_q'], pos, (D,), (BF16,))
            lam_init = 0.8 - 0.6 * math.exp(-0.3 * l)
            o = _attn(q, kb, vb, kpb, vpb, P['diff_lambda'], row(P['diff_subln_g']), lam_init)
            w_o = P['diff_w_o']
        x = _ffn(x, s3, c3, g3, row(ng[4]), row(ng[5]), P['ffn_w_in'], P['ffn_w_out'], l, 1,
                 mixer=(o, g2, row(ng[3]), w_o))
    return (x, wkv[None], shift[None],
            k_new.reshape(B, T, N_HEADS_B, 2, HEAD_QK), v_new.reshape(B, T, N_HEADS_B, HEAD_V))


def kernel(x_prompt, x_sample, c_prompt, c_sample, state_wkv, state_shift, cache_k, cache_v, ada_w, ada_b, norm_g, ffn_w_in, ffn_w_out, rwkv_mu, rwkv_w_rkv, rwkv_w0, rwkv_w1, rwkv_w2, rwkv_a0, rwkv_a1, rwkv_a2, rwkv_g1, rwkv_g2, rwkv_k_k, rwkv_k_a, rwkv_r_k, rwkv_ln_w, rwkv_ln_b, rwkv_w_o, kv_ada_w, kv_ada_b, kv_norm_g, kv_w, diff_w_q, diff_lambda, diff_subln_g, diff_w_o):
    Bp, _, D = x_prompt.shape
    Bs = x_sample.shape[0]
    H = D // RWKV_HEAD
    bf = lambda w: w.astype(BF16)
    P = dict(norm_g=norm_g, ffn_w_in=bf(ffn_w_in), ffn_w_out=bf(ffn_w_out),
             rwkv_mu=rwkv_mu[0], rwkv_w_rkv=bf(rwkv_w_rkv[0]), rwkv_w0=rwkv_w0[0], rwkv_w1=bf(rwkv_w1[0]),
             rwkv_w2=bf(rwkv_w2[0]), rwkv_a0=rwkv_a0[0], rwkv_a1=bf(rwkv_a1[0]), rwkv_a2=bf(rwkv_a2[0]),
             rwkv_g1=bf(rwkv_g1[0]), rwkv_g2=bf(rwkv_g2[0]), rwkv_k_k=rwkv_k_k[0], rwkv_k_a=rwkv_k_a[0],
             rwkv_r_k=rwkv_r_k[0], rwkv_ln_w=rwkv_ln_w[0], rwkv_ln_b=rwkv_ln_b[0], rwkv_w_o=bf(rwkv_w_o[0]),
             kv_norm_g=kv_norm_g, kv_w=bf(kv_w), diff_w_q=bf(diff_w_q[0]), diff_lambda=diff_lambda[0],
             diff_subln_g=diff_subln_g[0], diff_w_o=bf(diff_w_o[0]))

    c_all = jnp.concatenate([c_prompt, c_sample], axis=0)
    mods = _ada(c_all, ada_w, ada_b[:, None, :])
    kv_mod = _ada(c_all, kv_ada_w[None], kv_ada_b[None, None, :])[0]

    wkv0 = jnp.zeros((Bp, H, RWKV_HEAD, RWKV_HEAD), F32)
    shift0 = jnp.zeros((Bp, 1, D), F32)
    y_p, wkv_p, shift_p, k_p, v_p = _trunk(
        x_prompt, [mods[0, :Bp], mods[1, :Bp]], kv_mod[:Bp], 0, wkv0, shift0, None, None, P)
    y_s, wkv_s, shift_s, k_s, v_s = _trunk(
        x_sample, [mods[0, Bp:], mods[1, Bp:]], kv_mod[Bp:], cache_k.shape[1],
        state_wkv[0], state_shift[0], cache_k, cache_v, P)
    return (y_p, y_s, wkv_p, shift_p, k_p, v_p, wkv_s, shift_s, k_s, v_s)
```

```python
import functools
import math

import jax
import jax.numpy as jnp
from jax import lax
from jax.experimental import pallas as pl
from jax.experimental.pallas import tpu as pltpu

F32 = jnp.float32
BF16 = jnp.bfloat16
HI = lax.Precision.HIGHEST

CHUNK = 64
RWKV_HEAD = 64
GN_EPS = 64e-5
N_HEADS_B = 8
HEAD_QK = 64
HEAD_V = 128
ROT_DIM = HEAD_QK // 4
ROPE_THETA = 500000.0
ATTN_SCALE = HEAD_QK ** -0.5
LOG2E = math.log2(math.e)
NORM_EPS = 1e-6
NEG_INF = -1e30

LANES = 128
VMEM_LIMIT = 56 * 1024 * 1024


def _cparams(sem):
    return pltpu.CompilerParams(dimension_semantics=sem, vmem_limit_bytes=VMEM_LIMIT)


def _tile(B, T, rows):
    if T >= rows:
        assert T % rows == 0
        return 1, rows
    bb = min(B, max(1, rows // T))
    while B % bb:
        bb -= 1
    return bb, T


def _rms(x, g):
    return x * lax.rsqrt(jnp.mean(x * x, axis=-1, keepdims=True) + NORM_EPS) * g


def _sigmoid(x):
    return 1.0 / (1.0 + jnp.exp(-x))


def _dot(a, b):
    return jnp.dot(a, b, preferred_element_type=F32)


def _dot_hi(a, b):
    return jnp.dot(a, b, preferred_element_type=F32, precision=HI)


def _dot_nt_hi(a, b):
    return lax.dot_general(a, b, (((1,), (1,)), ((), ())), preferred_element_type=F32, precision=HI)


def _dot_tn_hi(a, b):
    return lax.dot_general(a, b, (((0,), (0,)), ((), ())), preferred_element_type=F32, precision=HI)


def _ada_kernel(c_ref, w_ref, b_ref, o_ref):
    c = c_ref[...]
    sc = (c * _sigmoid(c)).astype(BF16)
    o_ref[...] = _dot(sc, w_ref[...].astype(BF16)) + b_ref[...]


def _ada(c, w, b, tn=1024):
    L, D, N = w.shape
    M = c.shape[0]
    return pl.pallas_call(
        _ada_kernel,
        grid=(L, N // tn),
        in_specs=[pl.BlockSpec((M, D), lambda l, j: (0, 0)),
                  pl.BlockSpec((None, D, tn), lambda l, j: (l, 0, j)),
                  pl.BlockSpec((None, 1, tn), lambda l, j: (l, 0, j))],
        out_specs=pl.BlockSpec((None, M, tn), lambda l, j: (l, 0, j)),
        out_shape=jax.ShapeDtypeStruct((L, M, N), F32),
        compiler_params=_cparams(("parallel", "parallel")),
        name="ada",
    )(c, w, b)


def _ffn_kernel(*refs, tf, has_mixer):
    if has_mixer:
        a_ref, mg_ref, mng_ref, mw_ref = refs[:4]
        refs = refs[4:]
    x_ref, s_ref, c_ref, g_ref, ngpre_ref, ngpost_ref, wg_ref, wu_ref, wo_ref, o_ref = refs
    bb, tt, D = x_ref.shape
    x = x_ref[...]
    if has_mixer:
        a = a_ref[...].reshape(bb * tt, a_ref.shape[-1]).astype(BF16)
        x = x + mg_ref[...] * _rms(_dot(a, mw_ref[...]).reshape(bb, tt, D), mng_ref[...])
    h = (_rms(x, ngpre_ref[...]) * (1.0 + c_ref[...]) + s_ref[...]).reshape(bb * tt, D).astype(BF16)
    acc = None
    for j in range(wo_ref.shape[0] // tf):
        cols = slice(j * tf, (j + 1) * tf)
        gate = _dot(h, wg_ref[:, cols])
        up = _dot(h, wu_ref[:, cols])
        act = ((gate * _sigmoid(gate)) * up).astype(BF16)
        part = _dot(act, wo_ref[cols, :])
        acc = part if acc is None else acc + part
    o_ref[...] = x + 0.5 * g_ref[...] * _rms(acc.reshape(bb, tt, D), ngpost_ref[...])


def _ffn(x, s, c, g, ng_pre, ng_post, w_in, w_out, l, m, mixer=None, rows=512, tf=256):
    B, T, D = x.shape
    Fh = w_out.shape[2]
    bb, tt = _tile(B, T, rows)
    xspec = pl.BlockSpec((bb, tt, D), lambda b, i: (b, i, 0))
    mspec = pl.BlockSpec((bb, 1, D), lambda b, i: (b, 0, 0))
    gspec = pl.BlockSpec((1, D), lambda b, i: (0, 0))
    once = pl.Buffered(1)
    in_specs = [xspec, mspec, mspec, mspec, gspec, gspec,
                pl.BlockSpec((None, None, D, Fh), lambda b, i: (l, m, 0, 0), pipeline_mode=once),
                pl.BlockSpec((None, None, D, Fh), lambda b, i: (l, m, 0, 1), pipeline_mode=once),
                pl.BlockSpec((None, None, Fh, D), lambda b, i: (l, m, 0, 0), pipeline_mode=once)]
    args = (x, s, c, g, ng_pre, ng_post, w_in, w_in, w_out)
    if mixer is not None:
        a, mg, mng, mw = mixer
        in_specs = [pl.BlockSpec((bb, tt, a.shape[-1]), lambda b, i: (b, i, 0)), mspec, gspec,
                    pl.BlockSpec(mw.shape, lambda b, i: (0, 0), pipeline_mode=once)] + in_specs
        args = (a, mg, mng, mw) + args
    return pl.pallas_call(
        functools.partial(_ffn_kernel, tf=tf, has_mixer=mixer is not None),
        grid=(B // bb, T // tt),
        in_specs=in_specs,
        out_specs=xspec,
        out_shape=jax.ShapeDtypeStruct((B, T, D), F32),
        compiler_params=_cparams(("parallel", "parallel")),
        name="ffn",
    )(*args)


def _rwkv_pre_kernel(x_ref, s_ref, c_ref, ng_ref, shift_ref, mu_ref, wrkv_ref, w0_ref, w1_ref, w2_ref,
                     a0_ref, a1_ref, a2_ref, g1_ref, g2_ref,
                     r_ref, k_ref, v_ref, a_ref, lw_ref, g_ref, shift_out_ref, prev_scr):
    bb, tt, D = x_ref.shape

    @pl.when(pl.program_id(1) == 0)
    def _():
        prev_scr[...] = shift_ref[...]

    hm = _rms(x_ref[...], ng_ref[...]) * (1.0 + c_ref[...]) + s_ref[...]
    row = lax.broadcasted_iota(jnp.int32, hm.shape, 1)
    xprev = jnp.where(row == 0, prev_scr[...], pltpu.roll(hm, 1, axis=1))
    last = hm[:, tt - 1:tt, :]
    prev_scr[...] = last
    shift_out_ref[...] = last
    dx = xprev - hm

    def mix(j):
        return (hm + dx * mu_ref[j:j + 1, :]).reshape(bb * tt, D).astype(BF16)

    def out(ref, val):
        ref[...] = val.reshape(bb, tt, D)

    out(r_ref, _dot(mix(0), wrkv_ref[0]))
    out(k_ref, _dot(mix(2), wrkv_ref[1]))
    out(v_ref, _dot(mix(3), wrkv_ref[2]))
    z = w0_ref[...] + _dot(jnp.tanh(_dot(mix(1), w1_ref[...])).astype(BF16), w2_ref[...])
    nz = -z
    softplus = jnp.maximum(nz, 0.0) + jnp.log1p(jnp.exp(-jnp.abs(nz)))
    out(lw_ref, -jnp.exp(-softplus - 0.5))
    out(a_ref, _sigmoid(a0_ref[...] + _dot(_dot(mix(4), a1_ref[...]).astype(BF16), a2_ref[...])))
    out(g_ref, _dot(_sigmoid(_dot(mix(5), g1_ref[...])).astype(BF16), g2_ref[...]))


def _rwkv_pre(x, s, c, ng, shift_in, mu, wrkv, w0, w1, w2, a0, a1, a2, g1, g2, rows=512):
    B, T, D = x.shape
    bb, tt = _tile(B, T, rows)
    xspec = pl.BlockSpec((bb, tt, D), lambda b, i: (b, i, 0))
    mspec = pl.BlockSpec((bb, 1, D), lambda b, i: (b, 0, 0))

    def full(a):
        return pl.BlockSpec(a.shape, lambda b, i: (0,) * a.ndim, pipeline_mode=pl.Buffered(1))

    big = jax.ShapeDtypeStruct((B, T, D), F32)
    return pl.pallas_call(
        _rwkv_pre_kernel,
        grid=(B // bb, T // tt),
        in_specs=[xspec, mspec, mspec, full(ng), mspec, full(mu), full(wrkv), full(w0), full(w1), full(w2),
                  full(a0), full(a1), full(a2), full(g1), full(g2)],
        out_specs=[xspec] * 6 + [mspec],
        out_shape=[big] * 6 + [jax.ShapeDtypeStruct((B, 1, D), F32)],
        scratch_shapes=[pltpu.VMEM((bb, 1, D), F32)],
        compiler_params=_cparams(("parallel", "arbitrary")),
        name="rwkv_pre",
    )(x, s, c, ng, shift_in, mu, wrkv, w0, w1, w2, a0, a1, a2, g1, g2)


def _pieces(x, n):
    out = []
    for _ in range(n):
        p = x.astype(BF16)
        out.append(p)
        x = x - p.astype(F32)
    return out


def _pdot(a, b, dims):
    order = max(len(a), len(b))
    acc = None
    for i, ai in enumerate(a):
        for j, bj in enumerate(b):
            if i + j < order:
                t = lax.dot_general(ai, bj, (dims, ((), ())), preferred_element_type=F32)
                acc = t if acc is None else acc + t
    return acc


_NN = ((1,), (0,))
_NT = ((1,), (1,))
_TN = ((0,), (0,))

WKV_P_IN = 1
WKV_P_TRI = 1


def _wkv_kernel(r_ref, k_ref, v_ref, a_ref, lw_ref, g_ref, s0_ref, kk_ref, ka_ref, rk_ref, lnw_ref, lnb_ref,
                o_ref, s_ref):
    C = CHUNK
    L2 = 2 * LANES
    npairs = s_ref.shape[0]

    @pl.when(pl.program_id(1) == 0)
    def _():
        s_ref[...] = jnp.zeros_like(s_ref)
        for p in range(npairs):
            s_ref[p, :RWKV_HEAD, :RWKV_HEAD] = s0_ref[2 * p]
            s_ref[p, RWKV_HEAD:, RWKV_HEAD:] = s0_ref[2 * p + 1]

    ri = lax.broadcasted_iota(jnp.int32, (LANES, L2), 0)
    ci = lax.broadcasted_iota(jnp.int32, (LANES, L2), 1) % LANES
    same_head = (ri // C) == (ci // C)
    tril_incl = same_head & ((ci % C) <= (ri % C))
    tril_strict = same_head & ((ci % C) < (ri % C))
    ri1 = lax.broadcasted_iota(jnp.int32, (LANES, LANES), 0)
    ci1 = lax.broadcasted_iota(jnp.int32, (LANES, LANES), 1)
    head_ones = [((ri1 // C) == (ci1 // C)).astype(F32).astype(BF16)]
    eye = (ri1 == ci1).astype(F32)
    r64 = lax.broadcasted_iota(jnp.int32, (C, C), 0)
    c64 = lax.broadcasted_iota(jnp.int32, (C, C), 1)
    cumsum_mat = [(c64 <= r64).astype(F32).astype(BF16)]
    lane = lax.broadcasted_iota(jnp.int32, (C, LANES), 1)
    first_head = lane < RWKV_HEAD

    def stack(x):
        return jnp.concatenate([jnp.where(first_head, x, 0.0), jnp.where(first_head, 0.0, x)], axis=0)

    def level_mask(b):
        return ((ri1 // (2 * b)) == (ci1 // (2 * b))) & ((ri1 % (2 * b)) >= b) & ((ci1 % (2 * b)) < b)

    pairs = range(npairs)

    def pair(x, p):
        return x[:, p * LANES:(p + 1) * LANES]

    def head_sum(x):
        xs = jnp.concatenate([pair(x, p) for p in pairs], axis=0)
        ys = _pdot(_pieces(xs, 2), head_ones, _NN)
        return jnp.concatenate([ys[p * C:(p + 1) * C] for p in pairs], axis=1)

    def chunk(ci, carry):
        rows = pl.ds(pl.multiple_of(ci * C, C), C)
        r, k, v, a, lw = r_ref[rows, :], k_ref[rows, :], v_ref[rows, :], a_ref[rows, :], lw_ref[rows, :]
        kk = k * kk_ref[...]
        kk = kk / jnp.maximum(jnp.sqrt(head_sum(kk * kk)), 1e-12)
        kmod = k * (1.0 + (a - 1.0) * ka_ref[...])
        bvec = kk * a
        cum = _pdot(cumsum_mat, _pieces(lw, 3), _NN)
        total = cum[C - 1:C, :]
        inv = jnp.exp(-cum)
        rem = jnp.exp(total - cum)
        decay = jnp.exp(total)
        a_t, r_t = -kk * jnp.exp(cum - lw), r * jnp.exp(cum)
        b_t, k_t = bvec * inv, kmod * inv
        b_h, k_h = bvec * rem, kmod * rem

        def stack2(x, y, p):
            return _pieces(jnp.concatenate([stack(pair(x, p)), stack(pair(y, p))], axis=0), WKV_P_IN)

        ar2 = [stack2(a_t, r_t, p) for p in pairs]
        bk2 = [stack2(b_t, k_t, p) for p in pairs]
        bkh2 = [stack2(b_h, k_h, p) for p in pairs]
        v2 = [stack(pair(v, p)) for p in pairs]

        gram = [_pdot(ar2[p], bk2[p], _NT) for p in pairs]
        lab_lak = [jnp.where(tril_strict, gram[p][:LANES], 0.0) for p in pairs]
        lab = [t[:, :LANES] for t in lab_lak]
        lak = [_pieces(t[:, LANES:], WKV_P_TRI) for t in lab_lak]
        mrbk = [_pieces(jnp.where(tril_incl, gram[p][LANES:], 0.0), WKV_P_TRI) for p in pairs]

        x = [eye + jnp.where(level_mask(1), lab[p], 0.0) for p in pairs]
        for b in (2, 4, 8, 16, 32):
            mask = level_mask(b)
            xp = [_pieces(x[p], WKV_P_TRI) for p in pairs]
            t = [_pdot(xp[p], _pieces(jnp.where(mask, lab[p], 0.0), WKV_P_TRI), _NN) for p in pairs]
            x = [x[p] + _pdot(_pieces(t[p], WKV_P_TRI), xp[p], _NN) for p in pairs]

        s = [s_ref[p] for p in pairs]
        ars = [_pdot(ar2[p], _pieces(s[p], WKV_P_IN), _NT) for p in pairs]
        rhs = [ars[p][:LANES] + _pdot(lak[p], _pieces(v2[p], WKV_P_IN), _NN) for p in pairs]
        u2 = [_pdot(_pieces(x[p], WKV_P_TRI), _pieces(rhs[p], WKV_P_IN), _NN) for p in pairs]
        uv = [_pieces(jnp.concatenate([u2[p], v2[p]], axis=0), WKV_P_IN) for p in pairs]
        y2 = [ars[p][LANES:] + _pdot(mrbk[p], uv[p], _NN) for p in pairs]
        for p in pairs:
            s_ref[p] = s[p] * pair(decay, p) + _pdot(uv[p], bkh2[p], _TN)
        y = jnp.concatenate([t[:C] + t[C:] for t in y2], axis=1)

        mean = head_sum(y) * (1.0 / RWKV_HEAD)
        d = y - mean
        var = head_sum(d * d) * (1.0 / RWKV_HEAD)
        yn = d * lax.rsqrt(var + GN_EPS) * lnw_ref[...] + lnb_ref[...]
        bonus = head_sum(r * kmod * rk_ref[...]) * v
        o_ref[rows, :] = (yn + bonus) * g_ref[rows, :]
        return carry

    lax.fori_loop(0, r_ref.shape[0] // C, chunk, 0)


def _wkv(r, k, v, a, lw, g, s0, k_k, k_a, r_k, ln_w, ln_b):
    B, T, D = r.shape
    npairs = D // LANES
    rows = min(T, 4 * CHUNK)
    xspec = pl.BlockSpec((None, rows, D), lambda b, i: (b, i, 0))
    sspec = pl.BlockSpec((None, npairs, LANES, LANES), lambda b, i: (b, 0, 0, 0))
    pspec = pl.BlockSpec((1, D), lambda b, i: (0, 0))
    return pl.pallas_call(
        _wkv_kernel,
        grid=(B, T // rows),
        in_specs=[xspec] * 6 + [pl.BlockSpec((None,) + s0.shape[1:], lambda b, i: (b, 0, 0, 0))] + [pspec] * 5,
        out_specs=[xspec, sspec],
        out_shape=[jax.ShapeDtypeStruct((B, T, D), F32),
                   jax.ShapeDtypeStruct((B, npairs, LANES, LANES), F32)],
        compiler_params=_cparams(("parallel", "arbitrary")),
        name="wkv",
    )(r, k, v, a, lw, g, s0, k_k, k_a, r_k, ln_w, ln_b)


def _state_from_blockdiag(s):
    B, P, N2, _ = s.shape
    N = N2 // 2
    return jnp.stack([s[:, :, :N, :N], s[:, :, N:, N:]], axis=2).reshape(B, 2 * P, N, N)


def _rope_tables(pos):
    half = ROT_DIM // 2
    inv = ROPE_THETA ** (-jnp.arange(half, dtype=F32) * 2.0 / ROT_DIM)
    ang = pos.astype(F32)[:, None] * inv[None, :]
    cos, sin = jnp.cos(ang), jnp.sin(ang)
    T = pos.shape[0]
    ones = jnp.ones((T, HEAD_QK - ROT_DIM), F32)
    zeros = jnp.zeros((T, HEAD_QK - ROT_DIM), F32)
    zh = jnp.zeros((T, half), F32)
    c = jnp.concatenate([cos, cos, ones], axis=-1)
    s1 = jnp.concatenate([-sin, zh, zeros], axis=-1)
    s2 = jnp.concatenate([zh, sin, zeros], axis=-1)
    return tuple(jnp.concatenate([t, t], axis=-1) for t in (c, s1, s2))


def _rope_rows(y, c, s1, s2):
    half = ROT_DIM // 2
    outs = []
    for p in range(y.shape[-1] // LANES):
        xs = y[:, p * LANES:(p + 1) * LANES]
        outs.append(xs * c + pltpu.roll(xs, LANES - half, axis=1) * s1 + pltpu.roll(xs, half, axis=1) * s2)
    return jnp.concatenate(outs, axis=-1)


def _nmm_kv_kernel(x_ref, s_ref, c_ref, ng_ref, w_ref, tc_ref, ts1_ref, ts2_ref,
                   k_ref, v_ref, kb_ref, vb_ref):
    bb, tt, D = x_ref.shape
    h = _rms(x_ref[...], ng_ref[...]) * (1.0 + c_ref[...]) + s_ref[...]
    kv = _dot(h.reshape(bb * tt, D).astype(BF16), w_ref[...])
    nk = k_ref.shape[-1]
    k = _rope_rows(kv[:, :nk], tc_ref[...], ts1_ref[...], ts2_ref[...]).reshape(bb, tt, nk)
    v = kv[:, nk:].reshape(bb, tt, v_ref.shape[-1])
    k_ref[...] = k
    v_ref[...] = v
    kb_ref[...] = k.astype(BF16)
    vb_ref[...] = v.astype(BF16)


def _nmm_q_kernel(x_ref, s_ref, c_ref, ng_ref, w_ref, tc_ref, ts1_ref, ts2_ref, q_ref):
    bb, tt, D = x_ref.shape
    h = _rms(x_ref[...], ng_ref[...]) * (1.0 + c_ref[...]) + s_ref[...]
    q = _dot(h.reshape(bb * tt, D).astype(BF16), w_ref[...])
    q = _rope_rows(q, tc_ref[...], ts1_ref[...], ts2_ref[...]) * (ATTN_SCALE * LOG2E)
    q_ref[...] = q.reshape(bb, tt, q_ref.shape[-1]).astype(BF16)


def _nmm_rope(body, x, s, c, ng, w, pos, out_widths, out_dtypes, rows=512):
    B, T, D = x.shape
    bb, tt = _tile(B, T, rows)
    tabs = _rope_tables(pos)
    if bb > 1:
        tabs = tuple(jnp.tile(t, (bb, 1)) for t in tabs)
        tspec = pl.BlockSpec((bb * tt, LANES), lambda b, i: (0, 0))
    else:
        tspec = pl.BlockSpec((tt, LANES), lambda b, i: (i, 0))
    xspec = pl.BlockSpec((bb, tt, D), lambda b, i: (b, i, 0))
    mspec = pl.BlockSpec((bb, 1, D), lambda b, i: (b, 0, 0))
    return pl.pallas_call(
        body,
        grid=(B // bb, T // tt),
        in_specs=[xspec, mspec, mspec, pl.BlockSpec((1, D), lambda b, i: (0, 0)),
                  pl.BlockSpec(w.shape, lambda b, i: (0, 0)), tspec, tspec, tspec],
        out_specs=[pl.BlockSpec((bb, tt, n), lambda b, i: (b, i, 0)) for n in out_widths],
        out_shape=[jax.ShapeDtypeStruct((B, T, n), dt) for n, dt in zip(out_widths, out_dtypes)],
        compiler_params=_cparams(("parallel", "parallel")),
        name="nmm_rope",
    )(x, s, c, ng, w, *tabs)


def _attn_kernel(*refs, q_off, tk, lam_init, has_past):
    if has_past:
        q_ref, kp_ref, vp_ref, kn_ref, vn_ref, lam_ref, sg_ref, o_ref, m_scr, acc_scr, p_scr = refs
    else:
        q_ref, kn_ref, vn_ref, lam_ref, sg_ref, o_ref, m_scr, acc_scr, p_scr = refs
        kp_ref, vp_ref = kn_ref, vn_ref
    tq = q_ref.shape[0]
    qi = pl.program_id(2)
    n_full = (q_off + qi * tq) // tk
    lane = lax.broadcasted_iota(jnp.int32, (tq, LANES), 1)
    q = q_ref[...]
    zero = jnp.zeros_like(q)
    qs = jnp.concatenate([jnp.where(lane < HEAD_QK, q, zero), jnp.where(lane < HEAD_QK, zero, q)], axis=0)
    own = pl.multiple_of(qi * tq, CHUNK)

    m_scr[...] = jnp.full_like(m_scr, NEG_INF)
    acc_scr[...] = jnp.zeros_like(acc_scr)

    def with_ones(v):
        return jnp.concatenate([v, jnp.ones_like(v)], axis=1)

    def scores(k_ref, start, size):
        return lax.dot_general(qs, k_ref[pl.ds(start, size), :], (((1,), (1,)), ((), ())),
                               preferred_element_type=F32)

    def softmax_step(s, size):
        m_old = m_scr[...]
        m_new = jnp.maximum(m_old, jnp.max(s, axis=-1, keepdims=True))
        m_wide = m_new[:, :size] if size < LANES else jnp.concatenate([m_new] * (size // LANES), axis=1)
        p = jnp.exp2(s - m_wide).astype(BF16)
        alpha = jnp.exp2(m_old - m_new)
        m_scr[...] = m_new
        return p, jnp.concatenate([alpha, alpha], axis=1)

    def finish(acc):
        lp = lam_ref[...]
        lam = (jnp.exp(jnp.sum(lp[0:1] * lp[1:2], axis=-1, keepdims=True))
               - jnp.exp(jnp.sum(lp[2:3] * lp[3:4], axis=-1, keepdims=True)) + lam_init)
        o = acc[:, :LANES] / acc[:, LANES:]
        o = o[:tq] - lam * o[tq:]
        o = o * lax.rsqrt(jnp.mean(o * o, axis=-1, keepdims=True) + NORM_EPS)
        o_ref[...] = o * sg_ref[...] * (1.0 - lam_init)

    p_scr[...] = jnp.zeros_like(p_scr)

    def pending_pv(j):
        start = pl.multiple_of(jnp.maximum(j - 1, 0) * tk, tk)
        return _dot(p_scr[...], with_ones(vp_ref[pl.ds(start, tk), :]))

    def full_tile(j, carry):
        s = scores(kp_ref, pl.multiple_of(j * tk, tk), tk)
        pv = pending_pv(j)
        p, alpha = softmax_step(s, tk)
        acc_scr[...] = alpha * (acc_scr[...] + pv)
        p_scr[...] = p
        return carry

    lax.fori_loop(0, n_full, full_tile, 0)
    s = scores(kn_ref, own, tq)
    pv = pending_pv(n_full)
    row = lax.broadcasted_iota(jnp.int32, (2 * tq, tq), 0)
    col = lax.broadcasted_iota(jnp.int32, (2 * tq, tq), 1)
    visible = (col // CHUNK) <= ((row % tq) // CHUNK)
    p, alpha = softmax_step(jnp.where(visible, s, NEG_INF), tq)
    finish(alpha * (acc_scr[...] + pv) + _dot(p, with_ones(vn_ref[pl.ds(own, tq), :])))


def _attn(q, k_new, v_new, k_past, v_past, lam_p, subln_g, lam_init, tile=512):
    B, T, W = q.shape
    H = W // LANES
    tq = min(T, tile)
    has_past = k_past is not None
    q_off = k_past.shape[1] if has_past else 0
    assert T % tq == 0 and q_off % tile == 0 and (T == tq if has_past else tq == tile)
    qspec = pl.BlockSpec((None, tq, LANES), lambda b, h, i: (b, i, h))

    def kspec(a):
        return pl.BlockSpec((None, a.shape[1], LANES), lambda b, h, i: (b, 0, h))

    kv = [k_past, v_past, k_new, v_new] if has_past else [k_new, v_new]
    scratch = [pltpu.VMEM((2 * tq, LANES), F32), pltpu.VMEM((2 * tq, 2 * LANES), F32),
               pltpu.VMEM((2 * tq, tile), BF16)]
    return pl.pallas_call(
        functools.partial(_attn_kernel, q_off=q_off, tk=tile, lam_init=lam_init, has_past=has_past),
        grid=(B, H, T // tq),
        in_specs=[qspec] + [kspec(a) for a in kv] +
                 [pl.BlockSpec(lam_p.shape, lambda b, h, i: (0, 0)), pl.BlockSpec((1, LANES), lambda b, h, i: (0, 0))],
        out_specs=qspec,
        out_shape=jax.ShapeDtypeStruct((B, T, W), F32),
        scratch_shapes=scratch,
        compiler_params=_cparams(("parallel", "parallel", "arbitrary")),
        name="attn",
    )(q, *kv, lam_p, subln_g)


def _trunk(x, mods, kv_mod, past_len, wkv0, shift0, k_past, v_past, P):
    B, T, D = x.shape
    pos = past_len + jnp.arange(T)

    def chunks(m, n):
        return [m[:, None, j * D:(j + 1) * D] for j in range(n)]

    row = lambda a: a.reshape(1, -1)
    for l in range(2):
        s1, c1, g1, s2, c2, g2, s3, c3, g3 = chunks(mods[l], 9)
        ng = P['norm_g'][l]
        if l == 1:
            kv_shift, kv_scale = chunks(kv_mod, 2)
            k_new, v_new, kb, vb = _nmm_rope(_nmm_kv_kernel, x, kv_shift, kv_scale, row(P['kv_norm_g']),
                                             P['kv_w'], pos, (D, D, D, D), (F32, F32, BF16, BF16))
            kpb = vpb = None
            if k_past is not None:
                kpb = k_past.reshape(B, past_len, D).astype(BF16)
                vpb = v_past.reshape(B, past_len, D).astype(BF16)
        x = _ffn(x, s1, c1, g1, row(ng[0]), row(ng[1]), P['ffn_w_in'], P['ffn_w_out'], l, 0)
        if l == 0:
            r, k, v, a, lw, g, shift = _rwkv_pre(
                x, s2, c2, row(ng[2]), shift0, P['rwkv_mu'], P['rwkv_w_rkv'], row(P['rwkv_w0']),
                P['rwkv_w1'], P['rwkv_w2'], row(P['rwkv_a0']), P['rwkv_a1'], P['rwkv_a2'],
                P['rwkv_g1'], P['rwkv_g2'])
            o, s_bd = _wkv(r, k, v, a, lw, g, wkv0, row(P['rwkv_k_k']),
                           row(P['rwkv_k_a']), row(P['rwkv_r_k']), row(P['rwkv_ln_w']), row(P['rwkv_ln_b']))
            wkv = _state_from_blockdiag(s_bd)
            w_o = P['rwkv_w_o']
        else:
            (q,) = _nmm_rope(_nmm_q_kernel, x, s2, c2, row(ng[2]), P['diff_w_q'], pos, (D,), (BF16,))
            lam_init = 0.8 - 0.6 * math.exp(-0.3 * l)
            o = _attn(q, kb, vb, kpb, vpb, P['diff_lambda'], row(P['diff_subln_g']), lam_init)
            w_o = P['diff_w_o']
        x = _ffn(x, s3, c3, g3, row(ng[4]), row(ng[5]), P['ffn_w_in'], P['ffn_w_out'], l, 1,
                 mixer=(o, g2, row(ng[3]), w_o))
    return (x, wkv[None], shift[None],
            k_new.reshape(B, T, N_HEADS_B, 2, HEAD_QK), v_new.reshape(B, T, N_HEADS_B, HEAD_V))


def kernel(x_prompt, x_sample, c_prompt, c_sample, state_wkv, state_shift, cache_k, cache_v, ada_w, ada_b, norm_g, ffn_w_in, ffn_w_out, rwkv_mu, rwkv_w_rkv, rwkv_w0, rwkv_w1, rwkv_w2, rwkv_a0, rwkv_a1, rwkv_a2, rwkv_g1, rwkv_g2, rwkv_k_k, rwkv_k_a, rwkv_r_k, rwkv_ln_w, rwkv_ln_b, rwkv_w_o, kv_ada_w, kv_ada_b, kv_norm_g, kv_w, diff_w_q, diff_lambda, diff_subln_g, diff_w_o):
    Bp, _, D = x_prompt.shape
    Bs = x_sample.shape[0]
    H = D // RWKV_HEAD
    bf = lambda w: w.astype(BF16)
    P = dict(norm_g=norm_g, ffn_w_in=bf(ffn_w_in), ffn_w_out=bf(ffn_w_out),
             rwkv_mu=rwkv_mu[0], rwkv_w_rkv=bf(rwkv_w_rkv[0]), rwkv_w0=rwkv_w0[0], rwkv_w1=bf(rwkv_w1[0]),
             rwkv_w2=bf(rwkv_w2[0]), rwkv_a0=rwkv_a0[0], rwkv_a1=bf(rwkv_a1[0]), rwkv_a2=bf(rwkv_a2[0]),
             rwkv_g1=bf(rwkv_g1[0]), rwkv_g2=bf(rwkv_g2[0]), rwkv_k_k=rwkv_k_k[0], rwkv_k_a=rwkv_k_a[0],
             rwkv_r_k=rwkv_r_k[0], rwkv_ln_w=rwkv_ln_w[0], rwkv_ln_b=rwkv_ln_b[0], rwkv_w_o=bf(rwkv_w_o[0]),
             kv_norm_g=kv_norm_g, kv_w=bf(kv_w), diff_w_q=bf(diff_w_q[0]), diff_lambda=diff_lambda[0],
             diff_subln_g=diff_subln_g[0], diff_w_o=bf(diff_w_o[0]))

    c_all = jnp.concatenate([c_prompt, c_sample], axis=0)
    mods = _ada(c_all, ada_w, ada_b[:, None, :])
    kv_mod = _ada(c_all, kv_ada_w[None], kv_ada_b[None, None, :])[0]

    wkv0 = jnp.zeros((Bp, H, RWKV_HEAD, RWKV_HEAD), F32)
    shift0 = jnp.zeros((Bp, 1, D), F32)
    y_p, wkv_p, shift_p, k_p, v_p = _trunk(
        x_prompt, [mods[0, :Bp], mods[1, :Bp]], kv_mod[:Bp], 0, wkv0, shift0, None, None, P)
    y_s, wkv_s, shift_s, k_s, v_s = _trunk(
        x_sample, [mods[0, Bp:], mods[1, Bp:]], kv_mod[Bp:], cache_k.shape[1],
        state_wkv[0], state_shift[0], cache_k, cache_v, P)
    return (y_p, y_s, wkv_p, shift_p, k_p, v_p, wkv_s, shift_s, k_s, v_s)
```

```python
import functools
import math

import jax
import jax.numpy as jnp
from jax import lax
from jax.experimental import pallas as pl
from jax.experimental.pallas import tpu as pltpu

F32 = jnp.float32
BF16 = jnp.bfloat16
HI = lax.Precision.HIGHEST

CHUNK = 64
RWKV_HEAD = 64
GN_EPS = 64e-5
N_HEADS_B = 8
HEAD_QK = 64
HEAD_V = 128
ROT_DIM = HEAD_QK // 4
ROPE_THETA = 500000.0
ATTN_SCALE = HEAD_QK ** -0.5
LOG2E = math.log2(math.e)
NORM_EPS = 1e-6
NEG_INF = -1e30

LANES = 128
VMEM_LIMIT = 56 * 1024 * 1024


def _cparams(sem):
    return pltpu.CompilerParams(dimension_semantics=sem, vmem_limit_bytes=VMEM_LIMIT)


def _tile(B, T, rows):
    if T >= rows:
        assert T % rows == 0
        return 1, rows
    bb = min(B, max(1, rows // T))
    while B % bb:
        bb -= 1
    return bb, T


def _rms(x, g):
    return x * lax.rsqrt(jnp.mean(x * x, axis=-1, keepdims=True) + NORM_EPS) * g


def _sigmoid(x):
    return 1.0 / (1.0 + jnp.exp(-x))


def _dot(a, b):
    return jnp.dot(a, b, preferred_element_type=F32)


def _dot_hi(a, b):
    return jnp.dot(a, b, preferred_element_type=F32, precision=HI)


def _dot_nt_hi(a, b):
    return lax.dot_general(a, b, (((1,), (1,)), ((), ())), preferred_element_type=F32, precision=HI)


def _dot_tn_hi(a, b):
    return lax.dot_general(a, b, (((0,), (0,)), ((), ())), preferred_element_type=F32, precision=HI)


def _ada_kernel(c_ref, w_ref, b_ref, o_ref):
    c = c_ref[...]
    sc = (c * _sigmoid(c)).astype(BF16)
    o_ref[...] = _dot(sc, w_ref[...].astype(BF16)) + b_ref[...]


def _ada(c, w, b, tn=1024):
    L, D, N = w.shape
    M = c.shape[0]
    return pl.pallas_call(
        _ada_kernel,
        grid=(L, N // tn),
        in_specs=[pl.BlockSpec((M, D), lambda l, j: (0, 0)),
                  pl.BlockSpec((None, D, tn), lambda l, j: (l, 0, j)),
                  pl.BlockSpec((None, 1, tn), lambda l, j: (l, 0, j))],
        out_specs=pl.BlockSpec((None, M, tn), lambda l, j: (l, 0, j)),
        out_shape=jax.ShapeDtypeStruct((L, M, N), F32),
        compiler_params=_cparams(("parallel", "parallel")),
        name="ada",
    )(c, w, b)


def _ffn_kernel(*refs, tf, has_mixer):
    if has_mixer:
        a_ref, mg_ref, mng_ref, mw_ref = refs[:4]
        refs = refs[4:]
    x_ref, s_ref, c_ref, g_ref, ngpre_ref, ngpost_ref, wg_ref, wu_ref, wo_ref, o_ref = refs
    bb, tt, D = x_ref.shape
    x = x_ref[...]
    if has_mixer:
        a = a_ref[...].reshape(bb * tt, a_ref.shape[-1]).astype(BF16)
        x = x + mg_ref[...] * _rms(_dot(a, mw_ref[...]).reshape(bb, tt, D), mng_ref[...])
    h = (_rms(x, ngpre_ref[...]) * (1.0 + c_ref[...]) + s_ref[...]).reshape(bb * tt, D).astype(BF16)
    acc = None
    for j in range(wo_ref.shape[0] // tf):
        cols = slice(j * tf, (j + 1) * tf)
        gate = _dot(h, wg_ref[:, cols])
        up = _dot(h, wu_ref[:, cols])
        act = ((gate * _sigmoid(gate)) * up).astype(BF16)
        part = _dot(act, wo_ref[cols, :])
        acc = part if acc is None else acc + part
    o_ref[...] = x + 0.5 * g_ref[...] * _rms(acc.reshape(bb, tt, D), ngpost_ref[...])


def _ffn(x, s, c, g, ng_pre, ng_post, w_in, w_out, l, m, mixer=None, rows=512, tf=256):
    B, T, D = x.shape
    Fh = w_out.shape[2]
    bb, tt = _tile(B, T, rows)
    xspec = pl.BlockSpec((bb, tt, D), lambda b, i: (b, i, 0))
    mspec = pl.BlockSpec((bb, 1, D), lambda b, i: (b, 0, 0))
    gspec = pl.BlockSpec((1, D), lambda b, i: (0, 0))
    once = pl.Buffered(1)
    in_specs = [xspec, mspec, mspec, mspec, gspec, gspec,
                pl.BlockSpec((None, None, D, Fh), lambda b, i: (l, m, 0, 0), pipeline_mode=once),
                pl.BlockSpec((None, None, D, Fh), lambda b, i: (l, m, 0, 1), pipeline_mode=once),
                pl.BlockSpec((None, None, Fh, D), lambda b, i: (l, m, 0, 0), pipeline_mode=once)]
    args = (x, s, c, g, ng_pre, ng_post, w_in, w_in, w_out)
    if mixer is not None:
        a, mg, mng, mw = mixer
        in_specs = [pl.BlockSpec((bb, tt, a.shape[-1]), lambda b, i: (b, i, 0)), mspec, gspec,
                    pl.BlockSpec(mw.shape, lambda b, i: (0, 0), pipeline_mode=once)] + in_specs
        args = (a, mg, mng, mw) + args
    return pl.pallas_call(
        functools.partial(_ffn_kernel, tf=tf, has_mixer=mixer is not None),
        grid=(B // bb, T // tt),
        in_specs=in_specs,
        out_specs=xspec,
        out_shape=jax.ShapeDtypeStruct((B, T, D), F32),
        compiler_params=_cparams(("parallel", "parallel")),
        name="ffn",
    )(*args)


def _rwkv_pre_kernel(x_ref, s_ref, c_ref, ng_ref, shift_ref, mu_ref, wrkv_ref, w0_ref, w1_ref, w2_ref,
                     a0_ref, a1_ref, a2_ref, g1_ref, g2_ref,
                     r_ref, k_ref, v_ref, a_ref, lw_ref, g_ref, shift_out_ref, prev_scr):
    bb, tt, D = x_ref.shape

    @pl.when(pl.program_id(1) == 0)
    def _():
        prev_scr[...] = shift_ref[...]

    hm = _rms(x_ref[...], ng_ref[...]) * (1.0 + c_ref[...]) + s_ref[...]
    row = lax.broadcasted_iota(jnp.int32, hm.shape, 1)
    xprev = jnp.where(row == 0, prev_scr[...], pltpu.roll(hm, 1, axis=1))
    last = hm[:, tt - 1:tt, :]
    prev_scr[...] = last
    shift_out_ref[...] = last
    dx = xprev - hm

    def mix(j):
        return (hm + dx * mu_ref[j:j + 1, :]).reshape(bb * tt, D).astype(BF16)

    def out(ref, val):
        ref[...] = val.reshape(bb, tt, D)

    out(r_ref, _dot(mix(0), wrkv_ref[0]))
    out(k_ref, _dot(mix(2), wrkv_ref[1]))
    out(v_ref, _dot(mix(3), wrkv_ref[2]))
    z = w0_ref[...] + _dot(jnp.tanh(_dot(mix(1), w1_ref[...])).astype(BF16), w2_ref[...])
    nz = -z
    softplus = jnp.maximum(nz, 0.0) + jnp.log1p(jnp.exp(-jnp.abs(nz)))
    out(lw_ref, -jnp.exp(-softplus - 0.5))
    out(a_ref, _sigmoid(a0_ref[...] + _dot(_dot(mix(4), a1_ref[...]).astype(BF16), a2_ref[...])))
    out(g_ref, _dot(_sigmoid(_dot(mix(5), g1_ref[...])).astype(BF16), g2_ref[...]))


def _rwkv_pre(x, s, c, ng, shift_in, mu, wrkv, w0, w1, w2, a0, a1, a2, g1, g2, rows=512):
    B, T, D = x.shape
    bb, tt = _tile(B, T, rows)
    xspec = pl.BlockSpec((bb, tt, D), lambda b, i: (b, i, 0))
    mspec = pl.BlockSpec((bb, 1, D), lambda b, i: (b, 0, 0))

    def full(a):
        return pl.BlockSpec(a.shape, lambda b, i: (0,) * a.ndim, pipeline_mode=pl.Buffered(1))

    big = jax.ShapeDtypeStruct((B, T, D), F32)
    return pl.pallas_call(
        _rwkv_pre_kernel,
        grid=(B // bb, T // tt),
        in_specs=[xspec, mspec, mspec, full(ng), mspec, full(mu), full(wrkv), full(w0), full(w1), full(w2),
                  full(a0), full(a1), full(a2), full(g1), full(g2)],
        out_specs=[xspec] * 6 + [mspec],
        out_shape=[big] * 6 + [jax.ShapeDtypeStruct((B, 1, D), F32)],
        scratch_shapes=[pltpu.VMEM((bb, 1, D), F32)],
        compiler_params=_cparams(("parallel", "arbitrary")),
        name="rwkv_pre",
    )(x, s, c, ng, shift_in, mu, wrkv, w0, w1, w2, a0, a1, a2, g1, g2)


def _pieces(x, n):
    out = []
    for _ in range(n):
        p = x.astype(BF16)
        out.append(p)
        x = x - p.astype(F32)
    return out


def _pdot(a, b, dims):
    order = max(len(a), len(b))
    acc = None
    for i, ai in enumerate(a):
        for j, bj in enumerate(b):
            if i + j < order:
                t = lax.dot_general(ai, bj, (dims, ((), ())), preferred_element_type=F32)
                acc = t if acc is None else acc + t
    return acc


_NN = ((1,), (0,))
_NT = ((1,), (1,))
_TN = ((0,), (0,))

WKV_P_IN = 1
WKV_P_TRI = 1


def _wkv_kernel(r_ref, k_ref, v_ref, a_ref, lw_ref, g_ref, s0_ref, kk_ref, ka_ref, rk_ref, lnw_ref, lnb_ref,
                o_ref, s_ref):
    C = CHUNK
    L2 = 2 * LANES
    npairs = s_ref.shape[0]

    @pl.when(pl.program_id(1) == 0)
    def _():
        s_ref[...] = jnp.zeros_like(s_ref)
        for p in range(npairs):
            s_ref[p, :RWKV_HEAD, :RWKV_HEAD] = s0_ref[2 * p]
            s_ref[p, RWKV_HEAD:, RWKV_HEAD:] = s0_ref[2 * p + 1]

    ri = lax.broadcasted_iota(jnp.int32, (LANES, L2), 0)
    ci = lax.broadcasted_iota(jnp.int32, (LANES, L2), 1) % LANES
    same_head = (ri // C) == (ci // C)
    tril_incl = same_head & ((ci % C) <= (ri % C))
    tril_strict = same_head & ((ci % C) < (ri % C))
    ri1 = lax.broadcasted_iota(jnp.int32, (LANES, LANES), 0)
    ci1 = lax.broadcasted_iota(jnp.int32, (LANES, LANES), 1)
    head_ones = [((ri1 // C) == (ci1 // C)).astype(F32).astype(BF16)]
    eye = (ri1 == ci1).astype(F32)
    r64 = lax.broadcasted_iota(jnp.int32, (C, C), 0)
    c64 = lax.broadcasted_iota(jnp.int32, (C, C), 1)
    cumsum_mat = [(c64 <= r64).astype(F32).astype(BF16)]
    lane = lax.broadcasted_iota(jnp.int32, (C, LANES), 1)
    first_head = lane < RWKV_HEAD

    def stack(x):
        return jnp.concatenate([jnp.where(first_head, x, 0.0), jnp.where(first_head, 0.0, x)], axis=0)

    def level_mask(b):
        return ((ri1 // (2 * b)) == (ci1 // (2 * b))) & ((ri1 % (2 * b)) >= b) & ((ci1 % (2 * b)) < b)

    pairs = range(npairs)

    def pair(x, p):
        return x[:, p * LANES:(p + 1) * LANES]

    def head_sum(x):
        xs = jnp.concatenate([pair(x, p) for p in pairs], axis=0)
        ys = _pdot(_pieces(xs, 2), head_ones, _NN)
        return jnp.concatenate([ys[p * C:(p + 1) * C] for p in pairs], axis=1)

    def chunk(ci, carry):
        rows = pl.ds(pl.multiple_of(ci * C, C), C)
        r, k, v, a, lw = r_ref[rows, :], k_ref[rows, :], v_ref[rows, :], a_ref[rows, :], lw_ref[rows, :]
        kk = k * kk_ref[...]
        kk = kk / jnp.maximum(jnp.sqrt(head_sum(kk * kk)), 1e-12)
        kmod = k * (1.0 + (a - 1.0) * ka_ref[...])
        bvec = kk * a
        cum = _pdot(cumsum_mat, _pieces(lw, 3), _NN)
        total = cum[C - 1:C, :]
        inv = jnp.exp(-cum)
        rem = jnp.exp(total - cum)
        decay = jnp.exp(total)
        a_t, r_t = -kk * jnp.exp(cum - lw), r * jnp.exp(cum)
        b_t, k_t = bvec * inv, kmod * inv
        b_h, k_h = bvec * rem, kmod * rem

        def stack2(x, y, p):
            return _pieces(jnp.concatenate([stack(pair(x, p)), stack(pair(y, p))], axis=0), WKV_P_IN)

        ar2 = [stack2(a_t, r_t, p) for p in pairs]
        bk2 = [stack2(b_t, k_t, p) for p in pairs]
        bkh2 = [stack2(b_h, k_h, p) for p in pairs]
        v2 = [stack(pair(v, p)) for p in pairs]

        gram = [_pdot(ar2[p], bk2[p], _NT) for p in pairs]
        lab_lak = [jnp.where(tril_strict, gram[p][:LANES], 0.0) for p in pairs]
        lab = [t[:, :LANES] for t in lab_lak]
        lak = [_pieces(t[:, LANES:], WKV_P_TRI) for t in lab_lak]
        mrbk = [_pieces(jnp.where(tril_incl, gram[p][LANES:], 0.0), WKV_P_TRI) for p in pairs]

        x = [eye + jnp.where(level_mask(1), lab[p], 0.0) for p in pairs]
        for b in (2, 4, 8, 16, 32):
            mask = level_mask(b)
            xp = [_pieces(x[p], WKV_P_TRI) for p in pairs]
            t = [_pdot(xp[p], _pieces(jnp.where(mask, lab[p], 0.0), WKV_P_TRI), _NN) for p in pairs]
            x = [x[p] + _pdot(_pieces(t[p], WKV_P_TRI), xp[p], _NN) for p in pairs]

        s = [s_ref[p] for p in pairs]
        ars = [_pdot(ar2[p], _pieces(s[p], WKV_P_IN), _NT) for p in pairs]
        rhs = [ars[p][:LANES] + _pdot(lak[p], _pieces(v2[p], WKV_P_IN), _NN) for p in pairs]
        u2 = [_pdot(_pieces(x[p], WKV_P_TRI), _pieces(rhs[p], WKV_P_IN), _NN) for p in pairs]
        uv = [_pieces(jnp.concatenate([u2[p], v2[p]], axis=0), WKV_P_IN) for p in pairs]
        y2 = [ars[p][LANES:] + _pdot(mrbk[p], uv[p], _NN) for p in pairs]
        for p in pairs:
            s_ref[p] = s[p] * pair(decay, p) + _pdot(uv[p], bkh2[p], _TN)
        y = jnp.concatenate([t[:C] + t[C:] for t in y2], axis=1)

        mean = head_sum(y) * (1.0 / RWKV_HEAD)
        d = y - mean
        var = head_sum(d * d) * (1.0 / RWKV_HEAD)
        yn = d * lax.rsqrt(var + GN_EPS) * lnw_ref[...] + lnb_ref[...]
        bonus = head_sum(r * kmod * rk_ref[...]) * v
        o_ref[rows, :] = (yn + bonus) * g_ref[rows, :]
        return carry

    lax.fori_loop(0, r_ref.shape[0] // C, chunk, 0)


def _wkv(r, k, v, a, lw, g, s0, k_k, k_a, r_k, ln_w, ln_b):
    B, T, D = r.shape
    npairs = D // LANES
    rows = min(T, 4 * CHUNK)
    xspec = pl.BlockSpec((None, rows, D), lambda b, i: (b, i, 0))
    sspec = pl.BlockSpec((None, npairs, LANES, LANES), lambda b, i: (b, 0, 0, 0))
    pspec = pl.BlockSpec((1, D), lambda b, i: (0, 0))
    return pl.pallas_call(
        _wkv_kernel,
        grid=(B, T // rows),
        in_specs=[xspec] * 6 + [pl.BlockSpec((None,) + s0.shape[1:], lambda b, i: (b, 0, 0, 0))] + [pspec] * 5,
        out_specs=[xspec, sspec],
        out_shape=[jax.ShapeDtypeStruct((B, T, D), F32),
                   jax.ShapeDtypeStruct((B, npairs, LANES, LANES), F32)],
        compiler_params=_cparams(("parallel", "arbitrary")),
        name="wkv",
    )(r, k, v, a, lw, g, s0, k_k, k_a, r_k, ln_w, ln_b)


def _state_from_blockdiag(s):
    B, P, N2, _ = s.shape
    N = N2 // 2
    return jnp.stack([s[:, :, :N, :N], s[:, :, N:, N:]], axis=2).reshape(B, 2 * P, N, N)


def _rope_tables(pos):
    half = ROT_DIM // 2
    inv = ROPE_THETA ** (-jnp.arange(half, dtype=F32) * 2.0 / ROT_DIM)
    ang = pos.astype(F32)[:, None] * inv[None, :]
    cos, sin = jnp.cos(ang), jnp.sin(ang)
    T = pos.shape[0]
    ones = jnp.ones((T, HEAD_QK - ROT_DIM), F32)
    zeros = jnp.zeros((T, HEAD_QK - ROT_DIM), F32)
    zh = jnp.zeros((T, half), F32)
    c = jnp.concatenate([cos, cos, ones], axis=-1)
    s1 = jnp.concatenate([-sin, zh, zeros], axis=-1)
    s2 = jnp.concatenate([zh, sin, zeros], axis=-1)
    return tuple(jnp.concatenate([t, t], axis=-1) for t in (c, s1, s2))


def _rope_rows(y, c, s1, s2):
    half = ROT_DIM // 2
    outs = []
    for p in range(y.shape[-1] // LANES):
        xs = y[:, p * LANES:(p + 1) * LANES]
        outs.append(xs * c + pltpu.roll(xs, LANES - half, axis=1) * s1 + pltpu.roll(xs, half, axis=1) * s2)
    return jnp.concatenate(outs, axis=-1)


def _nmm_kv_kernel(x_ref, s_ref, c_ref, ng_ref, w_ref, tc_ref, ts1_ref, ts2_ref,
                   k_ref, v_ref, kb_ref, vb_ref):
    bb, tt, D = x_ref.shape
    h = _rms(x_ref[...], ng_ref[...]) * (1.0 + c_ref[...]) + s_ref[...]
    kv = _dot(h.reshape(bb * tt, D).astype(BF16), w_ref[...])
    nk = k_ref.shape[-1]
    k = _rope_rows(kv[:, :nk], tc_ref[...], ts1_ref[...], ts2_ref[...]).reshape(bb, tt, nk)
    v = kv[:, nk:].reshape(bb, tt, v_ref.shape[-1])
    k_ref[...] = k
    v_ref[...] = v
    kb_ref[...] = k.astype(BF16)
    vb_ref[...] = v.astype(BF16)


def _nmm_q_kernel(x_ref, s_ref, c_ref, ng_ref, w_ref, tc_ref, ts1_ref, ts2_ref, q_ref):
    bb, tt, D = x_ref.shape
    h = _rms(x_ref[...], ng_ref[...]) * (1.0 + c_ref[...]) + s_ref[...]
    q = _dot(h.reshape(bb * tt, D).astype(BF16), w_ref[...])
    q = _rope_rows(q, tc_ref[...], ts1_ref[...], ts2_ref[...]) * (ATTN_SCALE * LOG2E)
    q_ref[...] = q.reshape(bb, tt, q_ref.shape[-1]).astype(BF16)


def _nmm_rope(body, x, s, c, ng, w, pos, out_widths, out_dtypes, rows=512):
    B, T, D = x.shape
    bb, tt = _tile(B, T, rows)
    tabs = _rope_tables(pos)
    if bb > 1:
        tabs = tuple(jnp.tile(t, (bb, 1)) for t in tabs)
        tspec = pl.BlockSpec((bb * tt, LANES), lambda b, i: (0, 0))
    else:
        tspec = pl.BlockSpec((tt, LANES), lambda b, i: (i, 0))
    xspec = pl.BlockSpec((bb, tt, D), lambda b, i: (b, i, 0))
    mspec = pl.BlockSpec((bb, 1, D), lambda b, i: (b, 0, 0))
    return pl.pallas_call(
        body,
        grid=(B // bb, T // tt),
        in_specs=[xspec, mspec, mspec, pl.BlockSpec((1, D), lambda b, i: (0, 0)),
                  pl.BlockSpec(w.shape, lambda b, i: (0, 0)), tspec, tspec, tspec],
        out_specs=[pl.BlockSpec((bb, tt, n), lambda b, i: (b, i, 0)) for n in out_widths],
        out_shape=[jax.ShapeDtypeStruct((B, T, n), dt) for n, dt in zip(out_widths, out_dtypes)],
        compiler_params=_cparams(("parallel", "parallel")),
        name="nmm_rope",
    )(x, s, c, ng, w, *tabs)


def _attn_kernel(*refs, q_off, tk, lam_init, has_past):
    if has_past:
        q_ref, kp_ref, vp_ref, kn_ref, vn_ref, lam_ref, sg_ref, o_ref, m_scr, acc_scr, p_scr = refs
    else:
        q_ref, kn_ref, vn_ref, lam_ref, sg_ref, o_ref, m_scr, acc_scr, p_scr, s_scr = refs
        kp_ref, vp_ref = kn_ref, vn_ref
    tq = q_ref.shape[0]
    qi = pl.program_id(2)
    n_full = (q_off + qi * tq) // tk
    lane = lax.broadcasted_iota(jnp.int32, (tq, LANES), 1)
    q = q_ref[...]
    zero = jnp.zeros_like(q)
    qs = jnp.concatenate([jnp.where(lane < HEAD_QK, q, zero), jnp.where(lane < HEAD_QK, zero, q)], axis=0)
    own = pl.multiple_of(qi * tq, CHUNK)

    m_scr[...] = jnp.full_like(m_scr, NEG_INF)
    acc_scr[...] = jnp.zeros_like(acc_scr)

    def with_ones(v):
        return jnp.concatenate([v, jnp.ones_like(v)], axis=1)

    def scores(k_ref, start, size):
        return lax.dot_general(qs, k_ref[pl.ds(start, size), :], (((1,), (1,)), ((), ())),
                               preferred_element_type=F32)

    def softmax_step(s, size):
        m_old = m_scr[...]
        m_new = jnp.maximum(m_old, jnp.max(s, axis=-1, keepdims=True))
        m_wide = m_new[:, :size] if size < LANES else jnp.concatenate([m_new] * (size // LANES), axis=1)
        p = jnp.exp2(s - m_wide).astype(BF16)
        alpha = jnp.exp2(m_old - m_new)
        m_scr[...] = m_new
        return p, jnp.concatenate([alpha, alpha], axis=1)

    def finish(acc):
        lp = lam_ref[...]
        lam = (jnp.exp(jnp.sum(lp[0:1] * lp[1:2], axis=-1, keepdims=True))
               - jnp.exp(jnp.sum(lp[2:3] * lp[3:4], axis=-1, keepdims=True)) + lam_init)
        o = acc[:, :LANES] / acc[:, LANES:]
        o = o[:tq] - lam * o[tq:]
        o = o * lax.rsqrt(jnp.mean(o * o, axis=-1, keepdims=True) + NORM_EPS)
        o_ref[...] = o * sg_ref[...] * (1.0 - lam_init)

    p_scr[...] = jnp.zeros_like(p_scr)

    def pending_pv(j):
        start = pl.multiple_of(jnp.maximum(j - 1, 0) * tk, tk)
        return _dot(p_scr[...], with_ones(vp_ref[pl.ds(start, tk), :]))

    def full_tile(j, carry):
        s = scores(kp_ref, pl.multiple_of(j * tk, tk), tk)
        pv = pending_pv(j)
        p, alpha = softmax_step(s, tk)
        acc_scr[...] = alpha * (acc_scr[...] + pv)
        p_scr[...] = p
        return carry

    def own_mask():
        row = lax.broadcasted_iota(jnp.int32, (2 * tq, tq), 0)
        col = lax.broadcasted_iota(jnp.int32, (2 * tq, tq), 1)
        return (col // CHUNK) <= ((row % tq) // CHUNK)

    if has_past:
        lax.fori_loop(0, n_full, full_tile, 0)
        s = scores(kn_ref, own, tq)
        pv = pending_pv(n_full)
        p, alpha = softmax_step(jnp.where(own_mask(), s, NEG_INF), tq)
        finish(alpha * (acc_scr[...] + pv) + _dot(p, with_ones(vn_ref[pl.ds(own, tq), :])))
        return

    def tile_start(t):
        return pl.multiple_of(jnp.minimum(t, jnp.maximum(n_full - 1, 0)) * tk, tk)

    def step(t, slot):
        s_next = scores(kn_ref, tile_start(t + 1), tk)
        prev = pl.multiple_of(jnp.where(t == 0, n_full, t - 1) * tk, tk)
        pv = _dot(p_scr[...], with_ones(vn_ref[pl.ds(prev, tk), :]))
        p, alpha = softmax_step(s_scr[slot], tk)
        acc_scr[...] = alpha * (acc_scr[...] + pv)
        p_scr[...] = p
        s_scr[1 - slot] = s_next

    s_own = scores(kn_ref, own, tq)
    s_scr[0] = scores(kn_ref, tile_start(0), tk)
    p, _ = softmax_step(jnp.where(own_mask(), s_own, NEG_INF), tq)
    p_scr[...] = p

    def two_tiles(jj, carry):
        step(2 * jj, 0)

        @pl.when(jj >= 0)
        def _():
            step(2 * jj + 1, 1)

        return carry

    lax.fori_loop(0, n_full // 2, two_tiles, 0)

    @pl.when(n_full % 2 == 1)
    def _():
        step(n_full - 1, 0)

    last = pl.multiple_of(jnp.where(n_full == 0, n_full, n_full - 1) * tk, tk)
    finish(acc_scr[...] + _dot(p_scr[...], with_ones(vn_ref[pl.ds(last, tk), :])))


def _attn(q, k_new, v_new, k_past, v_past, lam_p, subln_g, lam_init, tile=512):
    B, T, W = q.shape
    H = W // LANES
    tq = min(T, tile)
    has_past = k_past is not None
    q_off = k_past.shape[1] if has_past else 0
    if has_past:
        tile = q_off
    assert T % tq == 0 and q_off % tile == 0 and (T == tq if has_past else tq == tile)
    qspec = pl.BlockSpec((None, tq, LANES), lambda b, h, i: (b, i, h))

    def kspec(a):
        return pl.BlockSpec((None, a.shape[1], LANES), lambda b, h, i: (b, 0, h))

    kv = [k_past, v_past, k_new, v_new] if has_past else [k_new, v_new]
    scratch = [pltpu.VMEM((2 * tq, LANES), F32), pltpu.VMEM((2 * tq, 2 * LANES), F32),
               pltpu.VMEM((2 * tq, tile), BF16)]
    if not has_past:
        scratch.append(pltpu.VMEM((2, 2 * tq, tile), F32))
    return pl.pallas_call(
        functools.partial(_attn_kernel, q_off=q_off, tk=tile, lam_init=lam_init, has_past=has_past),
        grid=(B, H, T // tq),
        in_specs=[qspec] + [kspec(a) for a in kv] +
                 [pl.BlockSpec(lam_p.shape, lambda b, h, i: (0, 0)), pl.BlockSpec((1, LANES), lambda b, h, i: (0, 0))],
        out_specs=qspec,
        out_shape=jax.ShapeDtypeStruct((B, T, W), F32),
        scratch_shapes=scratch,
        compiler_params=_cparams(("parallel", "parallel", "arbitrary")),
        name="attn",
    )(q, *kv, lam_p, subln_g)


def _trunk(x, mods, kv_mod, past_len, wkv0, shift0, k_past, v_past, P):
    B, T, D = x.shape
    pos = past_len + jnp.arange(T)

    def chunks(m, n):
        return [m[:, None, j * D:(j + 1) * D] for j in range(n)]

    row = lambda a: a.reshape(1, -1)
    for l in range(2):
        s1, c1, g1, s2, c2, g2, s3, c3, g3 = chunks(mods[l], 9)
        ng = P['norm_g'][l]
        if l == 1:
            kv_shift, kv_scale = chunks(kv_mod, 2)
            k_new, v_new, kb, vb = _nmm_rope(_nmm_kv_kernel, x, kv_shift, kv_scale, row(P['kv_norm_g']),
                                             P['kv_w'], pos, (D, D, D, D), (F32, F32, BF16, BF16))
            kpb = vpb = None
            if k_past is not None:
                kpb = k_past.reshape(B, past_len, D).astype(BF16)
                vpb = v_past.reshape(B, past_len, D).astype(BF16)
        x = _ffn(x, s1, c1, g1, row(ng[0]), row(ng[1]), P['ffn_w_in'], P['ffn_w_out'], l, 0)
        if l == 0:
            r, k, v, a, lw, g, shift = _rwkv_pre(
                x, s2, c2, row(ng[2]), shift0, P['rwkv_mu'], P['rwkv_w_rkv'], row(P['rwkv_w0']),
                P['rwkv_w1'], P['rwkv_w2'], row(P['rwkv_a0']), P['rwkv_a1'], P['rwkv_a2'],
                P['rwkv_g1'], P['rwkv_g2'])
            o, s_bd = _wkv(r, k, v, a, lw, g, wkv0, row(P['rwkv_k_k']),
                           row(P['rwkv_k_a']), row(P['rwkv_r_k']), row(P['rwkv_ln_w']), row(P['rwkv_ln_b']))
            wkv = _state_from_blockdiag(s_bd)
            w_o = P['rwkv_w_o']
        else:
            (q,) = _nmm_rope(_nmm_q_kernel, x, s2, c2, row(ng[2]), P['diff_w_q'], pos, (D,), (BF16,))
            lam_init = 0.8 - 0.6 * math.exp(-0.3 * l)
            o = _attn(q, kb, vb, kpb, vpb, P['diff_lambda'], row(P['diff_subln_g']), lam_init)
            w_o = P['diff_w_o']
        x = _ffn(x, s3, c3, g3, row(ng[4]), row(ng[5]), P['ffn_w_in'], P['ffn_w_out'], l, 1,
                 mixer=(o, g2, row(ng[3]), w_o))
    return (x, wkv[None], shift[None],
            k_new.reshape(B, T, N_HEADS_B, 2, HEAD_QK), v_new.reshape(B, T, N_HEADS_B, HEAD_V))


def kernel(x_prompt, x_sample, c_prompt, c_sample, state_wkv, state_shift, cache_k, cache_v, ada_w, ada_b, norm_g, ffn_w_in, ffn_w_out, rwkv_mu, rwkv_w_rkv, rwkv_w0, rwkv_w1, rwkv_w2, rwkv_a0, rwkv_a1, rwkv_a2, rwkv_g1, rwkv_g2, rwkv_k_k, rwkv_k_a, rwkv_r_k, rwkv_ln_w, rwkv_ln_b, rwkv_w_o, kv_ada_w, kv_ada_b, kv_norm_g, kv_w, diff_w_q, diff_lambda, diff_subln_g, diff_w_o):
    Bp, _, D = x_prompt.shape
    Bs = x_sample.shape[0]
    H = D // RWKV_HEAD
    bf = lambda w: w.astype(BF16)
    P = dict(norm_g=norm_g, ffn_w_in=bf(ffn_w_in), ffn_w_out=bf(ffn_w_out),
             rwkv_mu=rwkv_mu[0], rwkv_w_rkv=bf(rwkv_w_rkv[0]), rwkv_w0=rwkv_w0[0], rwkv_w1=bf(rwkv_w1[0]),
             rwkv_w2=bf(rwkv_w2[0]), rwkv_a0=rwkv_a0[0], rwkv_a1=bf(rwkv_a1[0]), rwkv_a2=bf(rwkv_a2[0]),
             rwkv_g1=bf(rwkv_g1[0]), rwkv_g2=bf(rwkv_g2[0]), rwkv_k_k=rwkv_k_k[0], rwkv_k_a=rwkv_k_a[0],
             rwkv_r_k=rwkv_r_k[0], rwkv_ln_w=rwkv_ln_w[0], rwkv_ln_b=rwkv_ln_b[0], rwkv_w_o=bf(rwkv_w_o[0]),
             kv_norm_g=kv_norm_g, kv_w=bf(kv_w), diff_w_q=bf(diff_w_q[0]), diff_lambda=diff_lambda[0],
             diff_subln_g=diff_subln_g[0], diff_w_o=bf(diff_w_o[0]))

    c_all = jnp.concatenate([c_prompt, c_sample], axis=0)
    mods = _ada(c_all, ada_w, ada_b[:, None, :])
    kv_mod = _ada(c_all, kv_ada_w[None], kv_ada_b[None, None, :])[0]

    wkv0 = jnp.zeros((Bp, H, RWKV_HEAD, RWKV_HEAD), F32)
    shift0 = jnp.zeros((Bp, 1, D), F32)
    y_p, wkv_p, shift_p, k_p, v_p = _trunk(
        x_prompt, [mods[0, :Bp], mods[1, :Bp]], kv_mod[:Bp], 0, wkv0, shift0, None, None, P)
    y_s, wkv_s, shift_s, k_s, v_s = _trunk(
        x_sample, [mods[0, Bp:], mods[1, Bp:]], kv_mod[Bp:], cache_k.shape[1],
        state_wkv[0], state_shift[0], cache_k, cache_v, P)
    return (y_p, y_s, wkv_p, shift_p, k_p, v_p, wkv_s, shift_s, k_s, v_s)
```

```python
import functools
import math

import jax
import jax.numpy as jnp
from jax import lax
from jax.experimental import pallas as pl
from jax.experimental.pallas import tpu as pltpu

F32 = jnp.float32
BF16 = jnp.bfloat16
HI = lax.Precision.HIGHEST

CHUNK = 64
RWKV_HEAD = 64
GN_EPS = 64e-5
N_HEADS_B = 8
HEAD_QK = 64
HEAD_V = 128
ROT_DIM = HEAD_QK // 4
ROPE_THETA = 500000.0
ATTN_SCALE = HEAD_QK ** -0.5
LOG2E = math.log2(math.e)
NORM_EPS = 1e-6
NEG_INF = -1e30

LANES = 128
VMEM_LIMIT = 56 * 1024 * 1024


def _cparams(sem):
    return pltpu.CompilerParams(dimension_semantics=sem, vmem_limit_bytes=VMEM_LIMIT)


def _tile(B, T, rows):
    if T >= rows:
        assert T % rows == 0
        return 1, rows
    bb = min(B, max(1, rows // T))
    while B % bb:
        bb -= 1
    return bb, T


def _rms(x, g):
    return x * lax.rsqrt(jnp.mean(x * x, axis=-1, keepdims=True) + NORM_EPS) * g


def _sigmoid(x):
    return 1.0 / (1.0 + jnp.exp(-x))


def _dot(a, b):
    return jnp.dot(a, b, preferred_element_type=F32)


def _dot_hi(a, b):
    return jnp.dot(a, b, preferred_element_type=F32, precision=HI)


def _dot_nt_hi(a, b):
    return lax.dot_general(a, b, (((1,), (1,)), ((), ())), preferred_element_type=F32, precision=HI)


def _dot_tn_hi(a, b):
    return lax.dot_general(a, b, (((0,), (0,)), ((), ())), preferred_element_type=F32, precision=HI)


def _ada_kernel(c_ref, w_ref, b_ref, o_ref):
    c = c_ref[...]
    sc = (c * _sigmoid(c)).astype(BF16)
    o_ref[...] = _dot(sc, w_ref[...].astype(BF16)) + b_ref[...]


def _ada(c, w, b, tn=1024):
    L, D, N = w.shape
    M = c.shape[0]
    return pl.pallas_call(
        _ada_kernel,
        grid=(L, N // tn),
        in_specs=[pl.BlockSpec((M, D), lambda l, j: (0, 0)),
                  pl.BlockSpec((None, D, tn), lambda l, j: (l, 0, j)),
                  pl.BlockSpec((None, 1, tn), lambda l, j: (l, 0, j))],
        out_specs=pl.BlockSpec((None, M, tn), lambda l, j: (l, 0, j)),
        out_shape=jax.ShapeDtypeStruct((L, M, N), F32),
        compiler_params=_cparams(("parallel", "parallel")),
        name="ada",
    )(c, w, b)


def _ffn_kernel(*refs, tf, has_mixer):
    if has_mixer:
        a_ref, mg_ref, mng_ref, mw_ref = refs[:4]
        refs = refs[4:]
    x_ref, s_ref, c_ref, g_ref, ngpre_ref, ngpost_ref, wg_ref, wu_ref, wo_ref, o_ref = refs
    bb, tt, D = x_ref.shape
    x = x_ref[...]
    if has_mixer:
        a = a_ref[...].reshape(bb * tt, a_ref.shape[-1]).astype(BF16)
        x = x + mg_ref[...] * _rms(_dot(a, mw_ref[...]).reshape(bb, tt, D), mng_ref[...])
    h = (_rms(x, ngpre_ref[...]) * (1.0 + c_ref[...]) + s_ref[...]).reshape(bb * tt, D).astype(BF16)
    acc = None
    for j in range(wo_ref.shape[0] // tf):
        cols = slice(j * tf, (j + 1) * tf)
        gate = _dot(h, wg_ref[:, cols])
        up = _dot(h, wu_ref[:, cols])
        act = ((gate * _sigmoid(gate)) * up).astype(BF16)
        part = _dot(act, wo_ref[cols, :])
        acc = part if acc is None else acc + part
    o_ref[...] = x + 0.5 * g_ref[...] * _rms(acc.reshape(bb, tt, D), ngpost_ref[...])


def _ffn(x, s, c, g, ng_pre, ng_post, w_in, w_out, l, m, mixer=None, rows=512, tf=256):
    B, T, D = x.shape
    Fh = w_out.shape[2]
    bb, tt = _tile(B, T, rows)
    xspec = pl.BlockSpec((bb, tt, D), lambda b, i: (b, i, 0))
    mspec = pl.BlockSpec((bb, 1, D), lambda b, i: (b, 0, 0))
    gspec = pl.BlockSpec((1, D), lambda b, i: (0, 0))
    once = pl.Buffered(1)
    in_specs = [xspec, mspec, mspec, mspec, gspec, gspec,
                pl.BlockSpec((None, None, D, Fh), lambda b, i: (l, m, 0, 0), pipeline_mode=once),
                pl.BlockSpec((None, None, D, Fh), lambda b, i: (l, m, 0, 1), pipeline_mode=once),
                pl.BlockSpec((None, None, Fh, D), lambda b, i: (l, m, 0, 0), pipeline_mode=once)]
    args = (x, s, c, g, ng_pre, ng_post, w_in, w_in, w_out)
    if mixer is not None:
        a, mg, mng, mw = mixer
        in_specs = [pl.BlockSpec((bb, tt, a.shape[-1]), lambda b, i: (b, i, 0)), mspec, gspec,
                    pl.BlockSpec(mw.shape, lambda b, i: (0, 0), pipeline_mode=once)] + in_specs
        args = (a, mg, mng, mw) + args
    return pl.pallas_call(
        functools.partial(_ffn_kernel, tf=tf, has_mixer=mixer is not None),
        grid=(B // bb, T // tt),
        in_specs=in_specs,
        out_specs=xspec,
        out_shape=jax.ShapeDtypeStruct((B, T, D), F32),
        compiler_params=_cparams(("parallel", "parallel")),
        name="ffn",
    )(*args)


def _rwkv_pre_kernel(x_ref, s_ref, c_ref, ng_ref, shift_ref, mu_ref, wrkv_ref, w0_ref, w1_ref, w2_ref,
                     a0_ref, a1_ref, a2_ref, g1_ref, g2_ref,
                     r_ref, k_ref, v_ref, a_ref, lw_ref, g_ref, shift_out_ref, prev_scr):
    bb, tt, D = x_ref.shape

    @pl.when(pl.program_id(1) == 0)
    def _():
        prev_scr[...] = shift_ref[...]

    hm = _rms(x_ref[...], ng_ref[...]) * (1.0 + c_ref[...]) + s_ref[...]
    row = lax.broadcasted_iota(jnp.int32, hm.shape, 1)
    xprev = jnp.where(row == 0, prev_scr[...], pltpu.roll(hm, 1, axis=1))
    last = hm[:, tt - 1:tt, :]
    prev_scr[...] = last
    shift_out_ref[...] = last
    dx = xprev - hm

    def mix(j):
        return (hm + dx * mu_ref[j:j + 1, :]).reshape(bb * tt, D).astype(BF16)

    def out(ref, val):
        ref[...] = val.reshape(bb, tt, D)

    out(r_ref, _dot(mix(0), wrkv_ref[0]))
    out(k_ref, _dot(mix(2), wrkv_ref[1]))
    out(v_ref, _dot(mix(3), wrkv_ref[2]))
    z = w0_ref[...] + _dot(jnp.tanh(_dot(mix(1), w1_ref[...])).astype(BF16), w2_ref[...])
    nz = -z
    softplus = jnp.maximum(nz, 0.0) + jnp.log1p(jnp.exp(-jnp.abs(nz)))
    out(lw_ref, -jnp.exp(-softplus - 0.5))
    out(a_ref, _sigmoid(a0_ref[...] + _dot(_dot(mix(4), a1_ref[...]).astype(BF16), a2_ref[...])))
    out(g_ref, _dot(_sigmoid(_dot(mix(5), g1_ref[...])).astype(BF16), g2_ref[...]))


def _rwkv_pre(x, s, c, ng, shift_in, mu, wrkv, w0, w1, w2, a0, a1, a2, g1, g2, rows=512):
    B, T, D = x.shape
    bb, tt = _tile(B, T, rows)
    xspec = pl.BlockSpec((bb, tt, D), lambda b, i: (b, i, 0))
    mspec = pl.BlockSpec((bb, 1, D), lambda b, i: (b, 0, 0))

    def full(a):
        return pl.BlockSpec(a.shape, lambda b, i: (0,) * a.ndim, pipeline_mode=pl.Buffered(1))

    big = jax.ShapeDtypeStruct((B, T, D), F32)
    return pl.pallas_call(
        _rwkv_pre_kernel,
        grid=(B // bb, T // tt),
        in_specs=[xspec, mspec, mspec, full(ng), mspec, full(mu), full(wrkv), full(w0), full(w1), full(w2),
                  full(a0), full(a1), full(a2), full(g1), full(g2)],
        out_specs=[xspec] * 6 + [mspec],
        out_shape=[big] * 6 + [jax.ShapeDtypeStruct((B, 1, D), F32)],
        scratch_shapes=[pltpu.VMEM((bb, 1, D), F32)],
        compiler_params=_cparams(("parallel", "arbitrary")),
        name="rwkv_pre",
    )(x, s, c, ng, shift_in, mu, wrkv, w0, w1, w2, a0, a1, a2, g1, g2)


def _pieces(x, n):
    out = []
    for _ in range(n):
        p = x.astype(BF16)
        out.append(p)
        x = x - p.astype(F32)
    return out


def _pdot(a, b, dims):
    order = max(len(a), len(b))
    acc = None
    for i, ai in enumerate(a):
        for j, bj in enumerate(b):
            if i + j < order:
                t = lax.dot_general(ai, bj, (dims, ((), ())), preferred_element_type=F32)
                acc = t if acc is None else acc + t
    return acc


_NN = ((1,), (0,))
_NT = ((1,), (1,))
_TN = ((0,), (0,))

WKV_P_IN = 1
WKV_P_TRI = 1


def _wkv_kernel(r_ref, k_ref, v_ref, a_ref, lw_ref, g_ref, s0_ref, kk_ref, ka_ref, rk_ref, lnw_ref, lnb_ref,
                o_ref, s_ref):
    C = CHUNK
    L2 = 2 * LANES
    npairs = s_ref.shape[0]

    @pl.when(pl.program_id(1) == 0)
    def _():
        s_ref[...] = jnp.zeros_like(s_ref)
        for p in range(npairs):
            s_ref[p, :RWKV_HEAD, :RWKV_HEAD] = s0_ref[2 * p]
            s_ref[p, RWKV_HEAD:, RWKV_HEAD:] = s0_ref[2 * p + 1]

    ri = lax.broadcasted_iota(jnp.int32, (LANES, L2), 0)
    ci = lax.broadcasted_iota(jnp.int32, (LANES, L2), 1) % LANES
    same_head = (ri // C) == (ci // C)
    tril_incl = same_head & ((ci % C) <= (ri % C))
    tril_strict = same_head & ((ci % C) < (ri % C))
    ri1 = lax.broadcasted_iota(jnp.int32, (LANES, LANES), 0)
    ci1 = lax.broadcasted_iota(jnp.int32, (LANES, LANES), 1)
    head_ones = [((ri1 // C) == (ci1 // C)).astype(F32).astype(BF16)]
    ri2 = lax.broadcasted_iota(jnp.int32, (L2, L2), 0)
    ci2 = lax.broadcasted_iota(jnp.int32, (L2, L2), 1)
    head_ones2 = [((ri2 // C) == (ci2 // C)).astype(F32).astype(BF16)]
    eye = (ri1 == ci1).astype(F32)
    r64 = lax.broadcasted_iota(jnp.int32, (C, C), 0)
    c64 = lax.broadcasted_iota(jnp.int32, (C, C), 1)
    cumsum_mat = [(c64 <= r64).astype(F32).astype(BF16)]
    lane = lax.broadcasted_iota(jnp.int32, (C, LANES), 1)
    first_head = lane < RWKV_HEAD

    def stack(x):
        return jnp.concatenate([jnp.where(first_head, x, 0.0), jnp.where(first_head, 0.0, x)], axis=0)

    def level_mask(b):
        return ((ri1 // (2 * b)) == (ci1 // (2 * b))) & ((ri1 % (2 * b)) >= b) & ((ci1 % (2 * b)) < b)

    pairs = range(npairs)

    def pair(x, p):
        return x[:, p * LANES:(p + 1) * LANES]

    def head_sum(x):
        xs = jnp.concatenate([pair(x, p) for p in pairs], axis=0)
        ys = _pdot(_pieces(xs, 2), head_ones, _NN)
        return jnp.concatenate([ys[p * C:(p + 1) * C] for p in pairs], axis=1)

    def head_sum2(x, y):
        xs = jnp.concatenate([jnp.concatenate([pair(x, p), pair(y, p)], axis=1) for p in pairs], axis=0)
        ys = _pdot(_pieces(xs, 2), head_ones2, _NN)
        return (jnp.concatenate([ys[p * C:(p + 1) * C, :LANES] for p in pairs], axis=1),
                jnp.concatenate([ys[p * C:(p + 1) * C, LANES:] for p in pairs], axis=1))

    def chunk(ci, carry):
        rows = pl.ds(pl.multiple_of(ci * C, C), C)
        r, k, v, a, lw = r_ref[rows, :], k_ref[rows, :], v_ref[rows, :], a_ref[rows, :], lw_ref[rows, :]
        kk = k * kk_ref[...]
        kmod = k * (1.0 + (a - 1.0) * ka_ref[...])
        kk_sq, rk_sum = head_sum2(kk * kk, r * kmod * rk_ref[...])
        kk = kk / jnp.maximum(jnp.sqrt(kk_sq), 1e-12)
        bvec = kk * a
        cum = _pdot(cumsum_mat, _pieces(lw, 3), _NN)
        total = cum[C - 1:C, :]
        inv = jnp.exp(-cum)
        rem = jnp.exp(total - cum)
        decay = jnp.exp(total)
        a_t, r_t = -kk * jnp.exp(cum - lw), r * jnp.exp(cum)
        b_t, k_t = bvec * inv, kmod * inv
        b_h, k_h = bvec * rem, kmod * rem

        def stack2(x, y, p):
            return _pieces(jnp.concatenate([stack(pair(x, p)), stack(pair(y, p))], axis=0), WKV_P_IN)

        ar2 = [stack2(a_t, r_t, p) for p in pairs]
        bk2 = [stack2(b_t, k_t, p) for p in pairs]
        bkh2 = [stack2(b_h, k_h, p) for p in pairs]
        v2 = [stack(pair(v, p)) for p in pairs]

        gram = [_pdot(ar2[p], bk2[p], _NT) for p in pairs]
        lab_lak = [jnp.where(tril_strict, gram[p][:LANES], 0.0) for p in pairs]
        lab = [t[:, :LANES] for t in lab_lak]
        lak = [_pieces(t[:, LANES:], WKV_P_TRI) for t in lab_lak]
        mrbk = [_pieces(jnp.where(tril_incl, gram[p][LANES:], 0.0), WKV_P_TRI) for p in pairs]

        x = [eye + jnp.where(level_mask(1), lab[p], 0.0) for p in pairs]
        for b in (2, 4, 8, 16, 32):
            mask = level_mask(b)
            xp = [_pieces(x[p], WKV_P_TRI) for p in pairs]
            t = [_pdot(xp[p], _pieces(jnp.where(mask, lab[p], 0.0), WKV_P_TRI), _NN) for p in pairs]
            x = [x[p] + _pdot(_pieces(t[p], WKV_P_TRI), xp[p], _NN) for p in pairs]

        s = [s_ref[p] for p in pairs]
        ars = [_pdot(ar2[p], _pieces(s[p], WKV_P_IN), _NT) for p in pairs]
        rhs = [ars[p][:LANES] + _pdot(lak[p], _pieces(v2[p], WKV_P_IN), _NN) for p in pairs]
        u2 = [_pdot(_pieces(x[p], WKV_P_TRI), _pieces(rhs[p], WKV_P_IN), _NN) for p in pairs]
        uv = [_pieces(jnp.concatenate([u2[p], v2[p]], axis=0), WKV_P_IN) for p in pairs]
        y2 = [ars[p][LANES:] + _pdot(mrbk[p], uv[p], _NN) for p in pairs]
        for p in pairs:
            s_ref[p] = s[p] * pair(decay, p) + _pdot(uv[p], bkh2[p], _TN)
        y = jnp.concatenate([t[:C] + t[C:] for t in y2], axis=1)

        mean = head_sum(y) * (1.0 / RWKV_HEAD)
        d = y - mean
        var = head_sum(d * d) * (1.0 / RWKV_HEAD)
        yn = d * lax.rsqrt(var + GN_EPS) * lnw_ref[...] + lnb_ref[...]
        bonus = rk_sum * v
        o_ref[rows, :] = (yn + bonus) * g_ref[rows, :]
        return carry

    lax.fori_loop(0, r_ref.shape[0] // C, chunk, 0)


def _wkv(r, k, v, a, lw, g, s0, k_k, k_a, r_k, ln_w, ln_b):
    B, T, D = r.shape
    npairs = D // LANES
    rows = min(T, 4 * CHUNK)
    xspec = pl.BlockSpec((None, rows, D), lambda b, i: (b, i, 0))
    sspec = pl.BlockSpec((None, npairs, LANES, LANES), lambda b, i: (b, 0, 0, 0))
    pspec = pl.BlockSpec((1, D), lambda b, i: (0, 0))
    return pl.pallas_call(
        _wkv_kernel,
        grid=(B, T // rows),
        in_specs=[xspec] * 6 + [pl.BlockSpec((None,) + s0.shape[1:], lambda b, i: (b, 0, 0, 0))] + [pspec] * 5,
        out_specs=[xspec, sspec],
        out_shape=[jax.ShapeDtypeStruct((B, T, D), F32),
                   jax.ShapeDtypeStruct((B, npairs, LANES, LANES), F32)],
        compiler_params=_cparams(("parallel", "arbitrary")),
        name="wkv",
    )(r, k, v, a, lw, g, s0, k_k, k_a, r_k, ln_w, ln_b)


def _state_from_blockdiag(s):
    B, P, N2, _ = s.shape
    N = N2 // 2
    return jnp.stack([s[:, :, :N, :N], s[:, :, N:, N:]], axis=2).reshape(B, 2 * P, N, N)


def _rope_angles(pos):
    half = ROT_DIM // 2
    inv = ROPE_THETA ** (-jnp.arange(half, dtype=F32) * 2.0 / ROT_DIM)
    ang = pos.astype(F32)[:, None] * inv[None, :]
    return jnp.concatenate([jnp.cos(ang), jnp.sin(ang)], axis=-1)


def _rope_rows(y, cs):
    half = ROT_DIM // 2
    r = lax.broadcasted_iota(jnp.int32, (ROT_DIM, 3 * LANES), 0)
    l = lax.broadcasted_iota(jnp.int32, (ROT_DIM, 3 * LANES), 1)
    table, d = l // LANES, l % HEAD_QK
    sel_c = (table == 0) & (r < half) & (d < ROT_DIM) & (d % half == r)
    sel_s1 = (table == 1) & (r >= half) & (d < half) & (d == r - half)
    sel_s2 = (table == 2) & (r >= half) & (d >= half) & (d < ROT_DIM) & (d - half == r - half)
    sel = (jnp.where(sel_c | sel_s2, 1.0, 0.0) - jnp.where(sel_s1, 1.0, 0.0)).astype(BF16)
    coef = _pdot(_pieces(cs, 3), [sel], _NN)
    lane = lax.broadcasted_iota(jnp.int32, (1, LANES), 1)
    c = coef[:, :LANES] + jnp.where(lane % HEAD_QK < ROT_DIM, 0.0, 1.0)
    s1, s2 = coef[:, LANES:2 * LANES], coef[:, 2 * LANES:]
    outs = []
    for p in range(y.shape[-1] // LANES):
        xs = y[:, p * LANES:(p + 1) * LANES]
        outs.append(xs * c + pltpu.roll(xs, LANES - half, axis=1) * s1 + pltpu.roll(xs, half, axis=1) * s2)
    return jnp.concatenate(outs, axis=-1)


def _nmm_kv_kernel(x_ref, s_ref, c_ref, ng_ref, w_ref, cs_ref, k_ref, v_ref, kb_ref, vb_ref):
    bb, tt, D = x_ref.shape
    h = _rms(x_ref[...], ng_ref[...]) * (1.0 + c_ref[...]) + s_ref[...]
    kv = _dot(h.reshape(bb * tt, D).astype(BF16), w_ref[...])
    nk = k_ref.shape[-1]
    k = _rope_rows(kv[:, :nk], cs_ref[...]).reshape(bb, tt, nk)
    v = kv[:, nk:].reshape(bb, tt, v_ref.shape[-1])
    k_ref[...] = k
    v_ref[...] = v
    kb_ref[...] = k.astype(BF16)
    vb_ref[...] = v.astype(BF16)


def _nmm_q_kernel(x_ref, s_ref, c_ref, ng_ref, w_ref, cs_ref, q_ref):
    bb, tt, D = x_ref.shape
    h = _rms(x_ref[...], ng_ref[...]) * (1.0 + c_ref[...]) + s_ref[...]
    q = _dot(h.reshape(bb * tt, D).astype(BF16), w_ref[...])
    q = _rope_rows(q, cs_ref[...]) * (ATTN_SCALE * LOG2E)
    q_ref[...] = q.reshape(bb, tt, q_ref.shape[-1]).astype(BF16)


def _nmm_rope(body, x, s, c, ng, w, pos, out_widths, out_dtypes, rows=512):
    B, T, D = x.shape
    bb, tt = _tile(B, T, rows)
    cs = _rope_angles(pos)
    if bb > 1:
        cs = jnp.tile(cs, (bb, 1))
        tspec = pl.BlockSpec((bb * tt, ROT_DIM), lambda b, i: (0, 0))
    else:
        tspec = pl.BlockSpec((tt, ROT_DIM), lambda b, i: (i, 0))
    xspec = pl.BlockSpec((bb, tt, D), lambda b, i: (b, i, 0))
    mspec = pl.BlockSpec((bb, 1, D), lambda b, i: (b, 0, 0))
    return pl.pallas_call(
        body,
        grid=(B // bb, T // tt),
        in_specs=[xspec, mspec, mspec, pl.BlockSpec((1, D), lambda b, i: (0, 0)),
                  pl.BlockSpec(w.shape, lambda b, i: (0, 0)), tspec],
        out_specs=[pl.BlockSpec((bb, tt, n), lambda b, i: (b, i, 0)) for n in out_widths],
        out_shape=[jax.ShapeDtypeStruct((B, T, n), dt) for n, dt in zip(out_widths, out_dtypes)],
        compiler_params=_cparams(("parallel", "parallel")),
        name="nmm_rope",
    )(x, s, c, ng, w, cs)


def _attn_kernel(*refs, q_off, tk, lam_init, has_past):
    if has_past:
        q_ref, kp_ref, vp_ref, kn_ref, vn_ref, lam_ref, sg_ref, o_ref, m_scr, acc_scr, p_scr = refs
    else:
        q_ref, kn_ref, vn_ref, lam_ref, sg_ref, o_ref, m_scr, acc_scr, p_scr, s_scr = refs
        kp_ref, vp_ref = kn_ref, vn_ref
    tq = q_ref.shape[0]
    qi = pl.program_id(2)
    n_full = (q_off + qi * tq) // tk
    lane = lax.broadcasted_iota(jnp.int32, (tq, LANES), 1)
    q = q_ref[...]
    zero = jnp.zeros_like(q)
    qs = jnp.concatenate([jnp.where(lane < HEAD_QK, q, zero), jnp.where(lane < HEAD_QK, zero, q)], axis=0)
    own = pl.multiple_of(qi * tq, CHUNK)

    m_scr[...] = jnp.full_like(m_scr, NEG_INF)
    acc_scr[...] = jnp.zeros_like(acc_scr)

    def with_ones(v):
        return jnp.concatenate([v, jnp.ones_like(v)], axis=1)

    def scores(k_ref, start, size):
        return lax.dot_general(qs, k_ref[pl.ds(start, size), :], (((1,), (1,)), ((), ())),
                               preferred_element_type=F32)

    def softmax_step(s, size):
        m_old = m_scr[...]
        m_new = jnp.maximum(m_old, jnp.max(s, axis=-1, keepdims=True))
        m_wide = m_new[:, :size] if size < LANES else jnp.concatenate([m_new] * (size // LANES), axis=1)
        p = jnp.exp2(s - m_wide).astype(BF16)
        alpha = jnp.exp2(m_old - m_new)
        m_scr[...] = m_new
        return p, jnp.concatenate([alpha, alpha], axis=1)

    def finish(acc):
        lp = lam_ref[...]
        lam = (jnp.exp(jnp.sum(lp[0:1] * lp[1:2], axis=-1, keepdims=True))
               - jnp.exp(jnp.sum(lp[2:3] * lp[3:4], axis=-1, keepdims=True)) + lam_init)
        o = acc[:, :LANES] / acc[:, LANES:]
        o = o[:tq] - lam * o[tq:]
        o = o * lax.rsqrt(jnp.mean(o * o, axis=-1, keepdims=True) + NORM_EPS)
        o_ref[...] = o * sg_ref[...] * (1.0 - lam_init)

    p_scr[...] = jnp.zeros_like(p_scr)

    def pending_pv(j):
        start = pl.multiple_of(jnp.maximum(j - 1, 0) * tk, tk)
        return _dot(p_scr[...], with_ones(vp_ref[pl.ds(start, tk), :]))

    def full_tile(j, carry):
        s = scores(kp_ref, pl.multiple_of(j * tk, tk), tk)
        pv = pending_pv(j)
        p, alpha = softmax_step(s, tk)
        acc_scr[...] = alpha * (acc_scr[...] + pv)
        p_scr[...] = p
        return carry

    def own_mask():
        row = lax.broadcasted_iota(jnp.int32, (2 * tq, tq), 0)
        col = lax.broadcasted_iota(jnp.int32, (2 * tq, tq), 1)
        return (col // CHUNK) <= ((row % tq) // CHUNK)

    if has_past:
        lax.fori_loop(0, n_full, full_tile, 0)
        s = scores(kn_ref, own, tq)
        pv = pending_pv(n_full)
        p, alpha = softmax_step(jnp.where(own_mask(), s, NEG_INF), tq)
        finish(alpha * (acc_scr[...] + pv) + _dot(p, with_ones(vn_ref[pl.ds(own, tq), :])))
        return

    def tile_start(t):
        return pl.multiple_of(jnp.minimum(t, jnp.maximum(n_full - 1, 0)) * tk, tk)

    def step(t, slot):
        s_next = scores(kn_ref, tile_start(t + 1), tk)
        prev = pl.multiple_of(jnp.where(t == 0, n_full, t - 1) * tk, tk)
        pv = _dot(p_scr[...], with_ones(vn_ref[pl.ds(prev, tk), :]))
        p, alpha = softmax_step(s_scr[slot], tk)
        acc_scr[...] = alpha * (acc_scr[...] + pv)
        p_scr[...] = p
        s_scr[1 - slot] = s_next

    s_own = scores(kn_ref, own, tq)
    s_scr[0] = scores(kn_ref, tile_start(0), tk)
    p, _ = softmax_step(jnp.where(own_mask(), s_own, NEG_INF), tq)
    p_scr[...] = p

    def two_tiles(jj, carry):
        step(2 * jj, 0)

        @pl.when(jj >= 0)
        def _():
            step(2 * jj + 1, 1)

        return carry

    lax.fori_loop(0, n_full // 2, two_tiles, 0)

    @pl.when(n_full % 2 == 1)
    def _():
        step(n_full - 1, 0)

    last = pl.multiple_of(jnp.where(n_full == 0, n_full, n_full - 1) * tk, tk)
    finish(acc_scr[...] + _dot(p_scr[...], with_ones(vn_ref[pl.ds(last, tk), :])))


def _attn(q, k_new, v_new, k_past, v_past, lam_p, subln_g, lam_init, tile=512):
    B, T, W = q.shape
    H = W // LANES
    tq = min(T, tile)
    has_past = k_past is not None
    q_off = k_past.shape[1] if has_past else 0
    if has_past:
        tile = q_off
    assert T % tq == 0 and q_off % tile == 0 and (T == tq if has_past else tq == tile)
    qspec = pl.BlockSpec((None, tq, LANES), lambda b, h, i: (b, i, h))

    def kspec(a):
        return pl.BlockSpec((None, a.shape[1], LANES), lambda b, h, i: (b, 0, h))

    kv = [k_past, v_past, k_new, v_new] if has_past else [k_new, v_new]
    scratch = [pltpu.VMEM((2 * tq, LANES), F32), pltpu.VMEM((2 * tq, 2 * LANES), F32),
               pltpu.VMEM((2 * tq, tile), BF16)]
    if not has_past:
        scratch.append(pltpu.VMEM((2, 2 * tq, tile), F32))
    return pl.pallas_call(
        functools.partial(_attn_kernel, q_off=q_off, tk=tile, lam_init=lam_init, has_past=has_past),
        grid=(B, H, T // tq),
        in_specs=[qspec] + [kspec(a) for a in kv] +
                 [pl.BlockSpec(lam_p.shape, lambda b, h, i: (0, 0)), pl.BlockSpec((1, LANES), lambda b, h, i: (0, 0))],
        out_specs=qspec,
        out_shape=jax.ShapeDtypeStruct((B, T, W), F32),
        scratch_shapes=scratch,
        compiler_params=_cparams(("parallel", "parallel", "arbitrary")),
        name="attn",
    )(q, *kv, lam_p, subln_g)


def _trunk(x, mods, kv_mod, past_len, wkv0, shift0, k_past, v_past, P):
    B, T, D = x.shape
    pos = past_len + jnp.arange(T)

    def chunks(m, n):
        return [m[:, None, j * D:(j + 1) * D] for j in range(n)]

    row = lambda a: a.reshape(1, -1)
    for l in range(2):
        s1, c1, g1, s2, c2, g2, s3, c3, g3 = chunks(mods[l], 9)
        ng = P['norm_g'][l]
        if l == 1:
            kv_shift, kv_scale = chunks(kv_mod, 2)
            k_new, v_new, kb, vb = _nmm_rope(_nmm_kv_kernel, x, kv_shift, kv_scale, row(P['kv_norm_g']),
                                             P['kv_w'], pos, (D, D, D, D), (F32, F32, BF16, BF16))
            kpb = vpb = None
            if k_past is not None:
                kpb = k_past.reshape(B, past_len, D).astype(BF16)
                vpb = v_past.reshape(B, past_len, D).astype(BF16)
        x = _ffn(x, s1, c1, g1, row(ng[0]), row(ng[1]), P['ffn_w_in'], P['ffn_w_out'], l, 0)
        if l == 0:
            r, k, v, a, lw, g, shift = _rwkv_pre(
                x, s2, c2, row(ng[2]), shift0, P['rwkv_mu'], P['rwkv_w_rkv'], row(P['rwkv_w0']),
                P['rwkv_w1'], P['rwkv_w2'], row(P['rwkv_a0']), P['rwkv_a1'], P['rwkv_a2'],
                P['rwkv_g1'], P['rwkv_g2'])
            o, s_bd = _wkv(r, k, v, a, lw, g, wkv0, row(P['rwkv_k_k']),
                           row(P['rwkv_k_a']), row(P['rwkv_r_k']), row(P['rwkv_ln_w']), row(P['rwkv_ln_b']))
            wkv = _state_from_blockdiag(s_bd)
            w_o = P['rwkv_w_o']
        else:
            (q,) = _nmm_rope(_nmm_q_kernel, x, s2, c2, row(ng[2]), P['diff_w_q'], pos, (D,), (BF16,))
            lam_init = 0.8 - 0.6 * math.exp(-0.3 * l)
            o = _attn(q, kb, vb, kpb, vpb, P['diff_lambda'], row(P['diff_subln_g']), lam_init)
            w_o = P['diff_w_o']
        x = _ffn(x, s3, c3, g3, row(ng[4]), row(ng[5]), P['ffn_w_in'], P['ffn_w_out'], l, 1,
                 mixer=(o, g2, row(ng[3]), w_o))
    return (x, wkv[None], shift[None],
            k_new.reshape(B, T, N_HEADS_B, 2, HEAD_QK), v_new.reshape(B, T, N_HEADS_B, HEAD_V))


def kernel(x_prompt, x_sample, c_prompt, c_sample, state_wkv, state_shift, cache_k, cache_v, ada_w, ada_b, norm_g, ffn_w_in, ffn_w_out, rwkv_mu, rwkv_w_rkv, rwkv_w0, rwkv_w1, rwkv_w2, rwkv_a0, rwkv_a1, rwkv_a2, rwkv_g1, rwkv_g2, rwkv_k_k, rwkv_k_a, rwkv_r_k, rwkv_ln_w, rwkv_ln_b, rwkv_w_o, kv_ada_w, kv_ada_b, kv_norm_g, kv_w, diff_w_q, diff_lambda, diff_subln_g, diff_w_o):
    Bp, _, D = x_prompt.shape
    Bs = x_sample.shape[0]
    H = D // RWKV_HEAD
    bf = lambda w: w.astype(BF16)
    P = dict(norm_g=norm_g, ffn_w_in=bf(ffn_w_in), ffn_w_out=bf(ffn_w_out),
             rwkv_mu=rwkv_mu[0], rwkv_w_rkv=bf(rwkv_w_rkv[0]), rwkv_w0=rwkv_w0[0], rwkv_w1=bf(rwkv_w1[0]),
             rwkv_w2=bf(rwkv_w2[0]), rwkv_a0=rwkv_a0[0], rwkv_a1=bf(rwkv_a1[0]), rwkv_a2=bf(rwkv_a2[0]),
             rwkv_g1=bf(rwkv_g1[0]), rwkv_g2=bf(rwkv_g2[0]), rwkv_k_k=rwkv_k_k[0], rwkv_k_a=rwkv_k_a[0],
             rwkv_r_k=rwkv_r_k[0], rwkv_ln_w=rwkv_ln_w[0], rwkv_ln_b=rwkv_ln_b[0], rwkv_w_o=bf(rwkv_w_o[0]),
             kv_norm_g=kv_norm_g, kv_w=bf(kv_w), diff_w_q=bf(diff_w_q[0]), diff_lambda=diff_lambda[0],
             diff_subln_g=diff_subln_g[0], diff_w_o=bf(diff_w_o[0]))

    c_all = jnp.concatenate([c_prompt, c_sample], axis=0)
    mods = _ada(c_all, ada_w, ada_b[:, None, :])
    kv_mod = _ada(c_all, kv_ada_w[None], kv_ada_b[None, None, :])[0]

    wkv0 = jnp.zeros((Bp, H, RWKV_HEAD, RWKV_HEAD), F32)
    shift0 = jnp.zeros((Bp, 1, D), F32)
    y_p, wkv_p, shift_p, k_p, v_p = _trunk(
        x_prompt, [mods[0, :Bp], mods[1, :Bp]], kv_mod[:Bp], 0, wkv0, shift0, None, None, P)
    y_s, wkv_s, shift_s, k_s, v_s = _trunk(
        x_sample, [mods[0, Bp:], mods[1, Bp:]], kv_mod[Bp:], cache_k.shape[1],
        state_wkv[0], state_shift[0], cache_k, cache_v, P)
    return (y_p, y_s, wkv_p, shift_p, k_p, v_p, wkv_s, shift_s, k_s, v_s)
```

```python
import functools
import math

import jax
import jax.numpy as jnp
from jax import lax
from jax.experimental import pallas as pl
from jax.experimental.pallas import tpu as pltpu

F32 = jnp.float32
BF16 = jnp.bfloat16

CHUNK = 64
RWKV_HEAD = 64
GN_EPS = 64e-5
N_HEADS_B = 8
HEAD_QK = 64
HEAD_V = 128
ROT_DIM = HEAD_QK // 4
ROPE_THETA = 500000.0
ATTN_SCALE = HEAD_QK ** -0.5
LOG2E = math.log2(math.e)
NORM_EPS = 1e-6
NEG_INF = -1e30

LANES = 128
VMEM_LIMIT = 56 * 1024 * 1024


def _cparams(sem):
    return pltpu.CompilerParams(dimension_semantics=sem, vmem_limit_bytes=VMEM_LIMIT)


def _tile(B, T, rows):
    if T >= rows:
        assert T % rows == 0
        return 1, rows
    bb = min(B, max(1, rows // T))
    while B % bb:
        bb -= 1
    return bb, T


def _rms(x, g):
    return x * lax.rsqrt(jnp.mean(x * x, axis=-1, keepdims=True) + NORM_EPS) * g


def _sigmoid(x):
    return 1.0 / (1.0 + jnp.exp(-x))


def _dot(a, b):
    return jnp.dot(a, b, preferred_element_type=F32)


def _ada_kernel(c_ref, w_ref, b_ref, o_ref):
    c = c_ref[...]
    sc = (c * _sigmoid(c)).astype(BF16)
    o_ref[...] = _dot(sc, w_ref[...].astype(BF16)) + b_ref[...]


def _ada(c, w, b, tn=1024):
    L, D, N = w.shape
    M = c.shape[0]
    return pl.pallas_call(
        _ada_kernel,
        grid=(L, N // tn),
        in_specs=[pl.BlockSpec((M, D), lambda l, j: (0, 0)),
                  pl.BlockSpec((None, D, tn), lambda l, j: (l, 0, j)),
                  pl.BlockSpec((None, 1, tn), lambda l, j: (l, 0, j))],
        out_specs=pl.BlockSpec((None, M, tn), lambda l, j: (l, 0, j)),
        out_shape=jax.ShapeDtypeStruct((L, M, N), F32),
        compiler_params=_cparams(("parallel", "parallel")),
        name="ada",
    )(c, w, b)


def _ffn_kernel(*refs, tf, has_mixer):
    if has_mixer:
        a_ref, mg_ref, mng_ref, mw_ref = refs[:4]
        refs = refs[4:]
    x_ref, s_ref, c_ref, g_ref, ngpre_ref, ngpost_ref, wg_ref, wu_ref, wo_ref, o_ref = refs
    bb, tt, D = x_ref.shape
    x = x_ref[...]
    if has_mixer:
        a = a_ref[...].reshape(bb * tt, a_ref.shape[-1]).astype(BF16)
        x = x + mg_ref[...] * _rms(_dot(a, mw_ref[...]).reshape(bb, tt, D), mng_ref[...])
    h = (_rms(x, ngpre_ref[...]) * (1.0 + c_ref[...]) + s_ref[...]).reshape(bb * tt, D).astype(BF16)
    acc = None
    for j in range(wo_ref.shape[0] // tf):
        cols = slice(j * tf, (j + 1) * tf)
        gate = _dot(h, wg_ref[:, cols])
        up = _dot(h, wu_ref[:, cols])
        act = ((gate * _sigmoid(gate)) * up).astype(BF16)
        part = _dot(act, wo_ref[cols, :])
        acc = part if acc is None else acc + part
    o_ref[...] = x + 0.5 * g_ref[...] * _rms(acc.reshape(bb, tt, D), ngpost_ref[...])


def _ffn(x, s, c, g, ng_pre, ng_post, w_in, w_out, l, m, mixer=None, rows=512, tf=256):
    B, T, D = x.shape
    Fh = w_out.shape[2]
    bb, tt = _tile(B, T, rows)
    xspec = pl.BlockSpec((bb, tt, D), lambda b, i: (b, i, 0))
    mspec = pl.BlockSpec((bb, 1, D), lambda b, i: (b, 0, 0))
    gspec = pl.BlockSpec((1, D), lambda b, i: (0, 0))
    once = pl.Buffered(1)
    in_specs = [xspec, mspec, mspec, mspec, gspec, gspec,
                pl.BlockSpec((None, None, D, Fh), lambda b, i: (l, m, 0, 0), pipeline_mode=once),
                pl.BlockSpec((None, None, D, Fh), lambda b, i: (l, m, 0, 1), pipeline_mode=once),
                pl.BlockSpec((None, None, Fh, D), lambda b, i: (l, m, 0, 0), pipeline_mode=once)]
    args = (x, s, c, g, ng_pre, ng_post, w_in, w_in, w_out)
    if mixer is not None:
        a, mg, mng, mw = mixer
        in_specs = [pl.BlockSpec((bb, tt, a.shape[-1]), lambda b, i: (b, i, 0)), mspec, gspec,
                    pl.BlockSpec(mw.shape, lambda b, i: (0, 0), pipeline_mode=once)] + in_specs
        args = (a, mg, mng, mw) + args
    return pl.pallas_call(
        functools.partial(_ffn_kernel, tf=tf, has_mixer=mixer is not None),
        grid=(B // bb, T // tt),
        in_specs=in_specs,
        out_specs=xspec,
        out_shape=jax.ShapeDtypeStruct((B, T, D), F32),
        compiler_params=_cparams(("parallel", "parallel")),
        name="ffn",
    )(*args)


def _rwkv_pre_kernel(x_ref, s_ref, c_ref, ng_ref, shift_ref, mu_ref, wrkv_ref, w0_ref, w1_ref, w2_ref,
                     a0_ref, a1_ref, a2_ref, g1_ref, g2_ref,
                     r_ref, k_ref, v_ref, a_ref, lw_ref, g_ref, shift_out_ref, prev_scr):
    bb, tt, D = x_ref.shape

    @pl.when(pl.program_id(1) == 0)
    def _():
        prev_scr[...] = shift_ref[...]

    hm = _rms(x_ref[...], ng_ref[...]) * (1.0 + c_ref[...]) + s_ref[...]
    row = lax.broadcasted_iota(jnp.int32, hm.shape, 1)
    xprev = jnp.where(row == 0, prev_scr[...], pltpu.roll(hm, 1, axis=1))
    last = hm[:, tt - 1:tt, :]
    prev_scr[...] = last
    shift_out_ref[...] = last
    dx = xprev - hm

    def mix(j):
        return (hm + dx * mu_ref[j:j + 1, :]).reshape(bb * tt, D).astype(BF16)

    def out(ref, val):
        ref[...] = val.reshape(bb, tt, D)

    out(r_ref, _dot(mix(0), wrkv_ref[0]))
    out(k_ref, _dot(mix(2), wrkv_ref[1]))
    out(v_ref, _dot(mix(3), wrkv_ref[2]))
    z = w0_ref[...] + _dot(jnp.tanh(_dot(mix(1), w1_ref[...])).astype(BF16), w2_ref[...])
    nz = -z
    softplus = jnp.maximum(nz, 0.0) + jnp.log1p(jnp.exp(-jnp.abs(nz)))
    out(lw_ref, -jnp.exp(-softplus - 0.5))
    out(a_ref, _sigmoid(a0_ref[...] + _dot(_dot(mix(4), a1_ref[...]).astype(BF16), a2_ref[...])))
    out(g_ref, _dot(_sigmoid(_dot(mix(5), g1_ref[...])).astype(BF16), g2_ref[...]))


def _rwkv_pre(x, s, c, ng, shift_in, mu, wrkv, w0, w1, w2, a0, a1, a2, g1, g2, rows=512):
    B, T, D = x.shape
    bb, tt = _tile(B, T, rows)
    xspec = pl.BlockSpec((bb, tt, D), lambda b, i: (b, i, 0))
    mspec = pl.BlockSpec((bb, 1, D), lambda b, i: (b, 0, 0))

    def full(a):
        return pl.BlockSpec(a.shape, lambda b, i: (0,) * a.ndim, pipeline_mode=pl.Buffered(1))

    big = jax.ShapeDtypeStruct((B, T, D), F32)
    return pl.pallas_call(
        _rwkv_pre_kernel,
        grid=(B // bb, T // tt),
        in_specs=[xspec, mspec, mspec, full(ng), mspec, full(mu), full(wrkv), full(w0), full(w1), full(w2),
                  full(a0), full(a1), full(a2), full(g1), full(g2)],
        out_specs=[xspec] * 6 + [mspec],
        out_shape=[big] * 6 + [jax.ShapeDtypeStruct((B, 1, D), F32)],
        scratch_shapes=[pltpu.VMEM((bb, 1, D), F32)],
        compiler_params=_cparams(("parallel", "arbitrary")),
        name="rwkv_pre",
    )(x, s, c, ng, shift_in, mu, wrkv, w0, w1, w2, a0, a1, a2, g1, g2)


def _pieces(x, n):
    out = []
    for _ in range(n):
        p = x.astype(BF16)
        out.append(p)
        x = x - p.astype(F32)
    return out


def _pdot(a, b, dims):
    order = max(len(a), len(b))
    acc = None
    for i, ai in enumerate(a):
        for j, bj in enumerate(b):
            if i + j < order:
                t = lax.dot_general(ai, bj, (dims, ((), ())), preferred_element_type=F32)
                acc = t if acc is None else acc + t
    return acc


_NN = ((1,), (0,))
_NT = ((1,), (1,))
_TN = ((0,), (0,))

WKV_P_IN = 1
WKV_P_TRI = 1


def _wkv_kernel(r_ref, k_ref, v_ref, a_ref, lw_ref, g_ref, s0_ref, kk_ref, ka_ref, rk_ref, lnw_ref, lnb_ref,
                o_ref, s_ref):
    C = CHUNK
    L2 = 2 * LANES
    npairs = s_ref.shape[0]

    @pl.when(pl.program_id(1) == 0)
    def _():
        s_ref[...] = jnp.zeros_like(s_ref)
        for p in range(npairs):
            s_ref[p, :RWKV_HEAD, :RWKV_HEAD] = s0_ref[2 * p]
            s_ref[p, RWKV_HEAD:, RWKV_HEAD:] = s0_ref[2 * p + 1]

    ri = lax.broadcasted_iota(jnp.int32, (LANES, L2), 0)
    ci = lax.broadcasted_iota(jnp.int32, (LANES, L2), 1) % LANES
    same_head = (ri // C) == (ci // C)
    tril_incl = same_head & ((ci % C) <= (ri % C))
    tril_strict = same_head & ((ci % C) < (ri % C))
    ri1 = lax.broadcasted_iota(jnp.int32, (LANES, LANES), 0)
    ci1 = lax.broadcasted_iota(jnp.int32, (LANES, LANES), 1)
    head_ones = [((ri1 // C) == (ci1 // C)).astype(F32).astype(BF16)]
    ri2 = lax.broadcasted_iota(jnp.int32, (L2, L2), 0)
    ci2 = lax.broadcasted_iota(jnp.int32, (L2, L2), 1)
    head_ones2 = [((ri2 // C) == (ci2 // C)).astype(F32).astype(BF16)]
    eye = (ri1 == ci1).astype(F32)
    r64 = lax.broadcasted_iota(jnp.int32, (C, C), 0)
    c64 = lax.broadcasted_iota(jnp.int32, (C, C), 1)
    cumsum_mat = [(c64 <= r64).astype(F32).astype(BF16)]
    lane = lax.broadcasted_iota(jnp.int32, (C, LANES), 1)
    first_head = lane < RWKV_HEAD

    def stack(x):
        return jnp.concatenate([jnp.where(first_head, x, 0.0), jnp.where(first_head, 0.0, x)], axis=0)

    def level_mask(b):
        return ((ri1 // (2 * b)) == (ci1 // (2 * b))) & ((ri1 % (2 * b)) >= b) & ((ci1 % (2 * b)) < b)

    pairs = range(npairs)

    def pair(x, p):
        return x[:, p * LANES:(p + 1) * LANES]

    def head_sum(x):
        xs = jnp.concatenate([pair(x, p) for p in pairs], axis=0)
        ys = _pdot(_pieces(xs, 2), head_ones, _NN)
        return jnp.concatenate([ys[p * C:(p + 1) * C] for p in pairs], axis=1)

    def head_sum2(x, y):
        xs = jnp.concatenate([jnp.concatenate([pair(x, p), pair(y, p)], axis=1) for p in pairs], axis=0)
        ys = _pdot(_pieces(xs, 2), head_ones2, _NN)
        return (jnp.concatenate([ys[p * C:(p + 1) * C, :LANES] for p in pairs], axis=1),
                jnp.concatenate([ys[p * C:(p + 1) * C, LANES:] for p in pairs], axis=1))

    def chunk(ci, carry):
        rows = pl.ds(pl.multiple_of(ci * C, C), C)
        r, k, v, a, lw = r_ref[rows, :], k_ref[rows, :], v_ref[rows, :], a_ref[rows, :], lw_ref[rows, :]
        kk = k * kk_ref[...]
        kmod = k * (1.0 + (a - 1.0) * ka_ref[...])
        kk_sq, rk_sum = head_sum2(kk * kk, r * kmod * rk_ref[...])
        kk = kk / jnp.maximum(jnp.sqrt(kk_sq), 1e-12)
        bvec = kk * a
        cum = _pdot(cumsum_mat, _pieces(lw, 3), _NN)
        total = cum[C - 1:C, :]
        inv = jnp.exp(-cum)
        rem = jnp.exp(total - cum)
        decay = jnp.exp(total)
        a_t, r_t = -kk * jnp.exp(cum - lw), r * jnp.exp(cum)
        b_t, k_t = bvec * inv, kmod * inv
        b_h, k_h = bvec * rem, kmod * rem

        def stack2(x, y, p):
            return _pieces(jnp.concatenate([stack(pair(x, p)), stack(pair(y, p))], axis=0), WKV_P_IN)

        ar2 = [stack2(a_t, r_t, p) for p in pairs]
        bk2 = [stack2(b_t, k_t, p) for p in pairs]
        bkh2 = [stack2(b_h, k_h, p) for p in pairs]
        v2 = [stack(pair(v, p)) for p in pairs]

        gram = [_pdot(ar2[p], bk2[p], _NT) for p in pairs]
        lab_lak = [jnp.where(tril_strict, gram[p][:LANES], 0.0) for p in pairs]
        lab = [t[:, :LANES] for t in lab_lak]
        lak = [_pieces(t[:, LANES:], WKV_P_TRI) for t in lab_lak]
        mrbk = [_pieces(jnp.where(tril_incl, gram[p][LANES:], 0.0), WKV_P_TRI) for p in pairs]

        x = [eye + jnp.where(level_mask(1), lab[p], 0.0) for p in pairs]
        for b in (2, 4, 8, 16, 32):
            mask = level_mask(b)
            xp = [_pieces(x[p], WKV_P_TRI) for p in pairs]
            t = [_pdot(xp[p], _pieces(jnp.where(mask, lab[p], 0.0), WKV_P_TRI), _NN) for p in pairs]
            x = [x[p] + _pdot(_pieces(t[p], WKV_P_TRI), xp[p], _NN) for p in pairs]

        s = [s_ref[p] for p in pairs]
        ars = [_pdot(ar2[p], _pieces(s[p], WKV_P_IN), _NT) for p in pairs]
        rhs = [ars[p][:LANES] + _pdot(lak[p], _pieces(v2[p], WKV_P_IN), _NN) for p in pairs]
        u2 = [_pdot(_pieces(x[p], WKV_P_TRI), _pieces(rhs[p], WKV_P_IN), _NN) for p in pairs]
        uv = [_pieces(jnp.concatenate([u2[p], v2[p]], axis=0), WKV_P_IN) for p in pairs]
        y2 = [ars[p][LANES:] + _pdot(mrbk[p], uv[p], _NN) for p in pairs]
        for p in pairs:
            s_ref[p] = s[p] * pair(decay, p) + _pdot(uv[p], bkh2[p], _TN)
        y = jnp.concatenate([t[:C] + t[C:] for t in y2], axis=1)

        mean = head_sum(y) * (1.0 / RWKV_HEAD)
        d = y - mean
        var = head_sum(d * d) * (1.0 / RWKV_HEAD)
        yn = d * lax.rsqrt(var + GN_EPS) * lnw_ref[...] + lnb_ref[...]
        bonus = rk_sum * v
        o_ref[rows, :] = (yn + bonus) * g_ref[rows, :]
        return carry

    lax.fori_loop(0, r_ref.shape[0] // C, chunk, 0)


def _wkv(r, k, v, a, lw, g, s0, k_k, k_a, r_k, ln_w, ln_b):
    B, T, D = r.shape
    npairs = D // LANES
    rows = min(T, 4 * CHUNK)
    xspec = pl.BlockSpec((None, rows, D), lambda b, i: (b, i, 0))
    sspec = pl.BlockSpec((None, npairs, LANES, LANES), lambda b, i: (b, 0, 0, 0))
    pspec = pl.BlockSpec((1, D), lambda b, i: (0, 0))
    return pl.pallas_call(
        _wkv_kernel,
        grid=(B, T // rows),
        in_specs=[xspec] * 6 + [pl.BlockSpec((None,) + s0.shape[1:], lambda b, i: (b, 0, 0, 0))] + [pspec] * 5,
        out_specs=[xspec, sspec],
        out_shape=[jax.ShapeDtypeStruct((B, T, D), F32),
                   jax.ShapeDtypeStruct((B, npairs, LANES, LANES), F32)],
        compiler_params=_cparams(("parallel", "arbitrary")),
        name="wkv",
    )(r, k, v, a, lw, g, s0, k_k, k_a, r_k, ln_w, ln_b)


def _state_from_blockdiag(s):
    B, P, N2, _ = s.shape
    N = N2 // 2
    return jnp.stack([s[:, :, :N, :N], s[:, :, N:, N:]], axis=2).reshape(B, 2 * P, N, N)


def _rope_angles(pos):
    half = ROT_DIM // 2
    inv = ROPE_THETA ** (-jnp.arange(half, dtype=F32) * 2.0 / ROT_DIM)
    ang = pos.astype(F32)[:, None] * inv[None, :]
    return jnp.concatenate([jnp.cos(ang), jnp.sin(ang)], axis=-1)


def _rope_rows(y, cs):
    half = ROT_DIM // 2
    r = lax.broadcasted_iota(jnp.int32, (ROT_DIM, 3 * LANES), 0)
    l = lax.broadcasted_iota(jnp.int32, (ROT_DIM, 3 * LANES), 1)
    table, d = l // LANES, l % HEAD_QK
    sel_c = (table == 0) & (r < half) & (d < ROT_DIM) & (d % half == r)
    sel_s1 = (table == 1) & (r >= half) & (d < half) & (d == r - half)
    sel_s2 = (table == 2) & (r >= half) & (d >= half) & (d < ROT_DIM) & (d - half == r - half)
    sel = (jnp.where(sel_c | sel_s2, 1.0, 0.0) - jnp.where(sel_s1, 1.0, 0.0)).astype(BF16)
    coef = _pdot(_pieces(cs, 3), [sel], _NN)
    lane = lax.broadcasted_iota(jnp.int32, (1, LANES), 1)
    c = coef[:, :LANES] + jnp.where(lane % HEAD_QK < ROT_DIM, 0.0, 1.0)
    s1, s2 = coef[:, LANES:2 * LANES], coef[:, 2 * LANES:]
    outs = []
    for p in range(y.shape[-1] // LANES):
        xs = y[:, p * LANES:(p + 1) * LANES]
        outs.append(xs * c + pltpu.roll(xs, LANES - half, axis=1) * s1 + pltpu.roll(xs, half, axis=1) * s2)
    return jnp.concatenate(outs, axis=-1)


def _nmm_kv_kernel(x_ref, s_ref, c_ref, ng_ref, w_ref, cs_ref, k_ref, v_ref, kb_ref, vb_ref):
    bb, tt, D = x_ref.shape
    h = _rms(x_ref[...], ng_ref[...]) * (1.0 + c_ref[...]) + s_ref[...]
    kv = _dot(h.reshape(bb * tt, D).astype(BF16), w_ref[...])
    nk = k_ref.shape[-1]
    k = _rope_rows(kv[:, :nk], cs_ref[...]).reshape(bb, tt, nk)
    v = kv[:, nk:].reshape(bb, tt, v_ref.shape[-1])
    k_ref[...] = k
    v_ref[...] = v
    kb_ref[...] = k.astype(BF16)
    vb_ref[...] = v.astype(BF16)


def _nmm_q_kernel(x_ref, s_ref, c_ref, ng_ref, w_ref, cs_ref, q_ref):
    bb, tt, D = x_ref.shape
    h = _rms(x_ref[...], ng_ref[...]) * (1.0 + c_ref[...]) + s_ref[...]
    q = _dot(h.reshape(bb * tt, D).astype(BF16), w_ref[...])
    q = _rope_rows(q, cs_ref[...]) * (ATTN_SCALE * LOG2E)
    q_ref[...] = q.reshape(bb, tt, q_ref.shape[-1]).astype(BF16)


def _nmm_rope(body, x, s, c, ng, w, pos, out_widths, out_dtypes, rows=512):
    B, T, D = x.shape
    bb, tt = _tile(B, T, rows)
    cs = _rope_angles(pos)
    if bb > 1:
        cs = jnp.tile(cs, (bb, 1))
        tspec = pl.BlockSpec((bb * tt, ROT_DIM), lambda b, i: (0, 0))
    else:
        tspec = pl.BlockSpec((tt, ROT_DIM), lambda b, i: (i, 0))
    xspec = pl.BlockSpec((bb, tt, D), lambda b, i: (b, i, 0))
    mspec = pl.BlockSpec((bb, 1, D), lambda b, i: (b, 0, 0))
    return pl.pallas_call(
        body,
        grid=(B // bb, T // tt),
        in_specs=[xspec, mspec, mspec, pl.BlockSpec((1, D), lambda b, i: (0, 0)),
                  pl.BlockSpec(w.shape, lambda b, i: (0, 0)), tspec],
        out_specs=[pl.BlockSpec((bb, tt, n), lambda b, i: (b, i, 0)) for n in out_widths],
        out_shape=[jax.ShapeDtypeStruct((B, T, n), dt) for n, dt in zip(out_widths, out_dtypes)],
        compiler_params=_cparams(("parallel", "parallel")),
        name="nmm_rope",
    )(x, s, c, ng, w, cs)


def _attn_kernel(*refs, q_off, tk, lam_init, has_past):
    if has_past:
        q_ref, kp_ref, vp_ref, kn_ref, vn_ref, lam_ref, sg_ref, o_ref, m_scr, acc_scr, p_scr = refs
    else:
        q_ref, kn_ref, vn_ref, lam_ref, sg_ref, o_ref, m_scr, acc_scr, p_scr, s_scr = refs
    tq = q_ref.shape[0]
    qi = pl.program_id(2)
    n_full = (q_off + qi * tq) // tk
    lane = lax.broadcasted_iota(jnp.int32, (tq, LANES), 1)
    q = q_ref[...]
    zero = jnp.zeros_like(q)
    qs = jnp.concatenate([jnp.where(lane < HEAD_QK, q, zero), jnp.where(lane < HEAD_QK, zero, q)], axis=0)
    own = pl.multiple_of(qi * tq, CHUNK)

    m_scr[...] = jnp.full_like(m_scr, NEG_INF)
    acc_scr[...] = jnp.zeros_like(acc_scr)

    def with_ones(v):
        return jnp.concatenate([v, jnp.ones_like(v)], axis=1)

    def scores(k_ref, start, size):
        return lax.dot_general(qs, k_ref[pl.ds(start, size), :], (((1,), (1,)), ((), ())),
                               preferred_element_type=F32)

    def softmax_step(s, size):
        m_old = m_scr[...]
        m_new = jnp.maximum(m_old, jnp.max(s, axis=-1, keepdims=True))
        m_wide = m_new[:, :size] if size < LANES else jnp.concatenate([m_new] * (size // LANES), axis=1)
        p = jnp.exp2(s - m_wide).astype(BF16)
        alpha = jnp.exp2(m_old - m_new)
        m_scr[...] = m_new
        return p, jnp.concatenate([alpha, alpha], axis=1)

    def finish(acc):
        lp = lam_ref[...]
        lam = (jnp.exp(jnp.sum(lp[0:1] * lp[1:2], axis=-1, keepdims=True))
               - jnp.exp(jnp.sum(lp[2:3] * lp[3:4], axis=-1, keepdims=True)) + lam_init)
        o = acc[:, :LANES] / acc[:, LANES:]
        o = o[:tq] - lam * o[tq:]
        o = o * lax.rsqrt(jnp.mean(o * o, axis=-1, keepdims=True) + NORM_EPS)
        o_ref[...] = o * sg_ref[...] * (1.0 - lam_init)

    def own_mask():
        row = lax.broadcasted_iota(jnp.int32, (2 * tq, tq), 0)
        col = lax.broadcasted_iota(jnp.int32, (2 * tq, tq), 1)
        return (col // CHUNK) <= ((row % tq) // CHUNK)

    if has_past:
        p_scr[...] = jnp.zeros_like(p_scr)

        def pending_pv(j):
            start = pl.multiple_of(jnp.maximum(j - 1, 0) * tk, tk)
            return _dot(p_scr[...], with_ones(vp_ref[pl.ds(start, tk), :]))

        def full_tile(j, carry):
            s = scores(kp_ref, pl.multiple_of(j * tk, tk), tk)
            pv = pending_pv(j)
            p, alpha = softmax_step(s, tk)
            acc_scr[...] = alpha * (acc_scr[...] + pv)
            p_scr[...] = p
            return carry

        lax.fori_loop(0, n_full, full_tile, 0)
        s = scores(kn_ref, own, tq)
        pv = pending_pv(n_full)
        p, alpha = softmax_step(jnp.where(own_mask(), s, NEG_INF), tq)
        finish(alpha * (acc_scr[...] + pv) + _dot(p, with_ones(vn_ref[pl.ds(own, tq), :])))
        return

    def tile_start(t):
        return pl.multiple_of(jnp.minimum(t, jnp.maximum(n_full - 1, 0)) * tk, tk)

    def step(t, slot):
        s_next = scores(kn_ref, tile_start(t + 1), tk)
        prev = pl.multiple_of(jnp.where(t == 0, n_full, t - 1) * tk, tk)
        pv = _dot(p_scr[...], with_ones(vn_ref[pl.ds(prev, tk), :]))
        p, alpha = softmax_step(s_scr[slot], tk)
        acc_scr[...] = alpha * (acc_scr[...] + pv)
        p_scr[...] = p
        s_scr[1 - slot] = s_next

    s_own = scores(kn_ref, own, tq)
    s_scr[0] = scores(kn_ref, tile_start(0), tk)
    p, _ = softmax_step(jnp.where(own_mask(), s_own, NEG_INF), tq)
    p_scr[...] = p

    def two_tiles(jj, carry):
        step(2 * jj, 0)

        @pl.when(jj >= 0)
        def _():
            step(2 * jj + 1, 1)

        return carry

    lax.fori_loop(0, n_full // 2, two_tiles, 0)

    @pl.when(n_full % 2 == 1)
    def _():
        step(n_full - 1, 0)

    last = pl.multiple_of(jnp.where(n_full == 0, n_full, n_full - 1) * tk, tk)
    finish(acc_scr[...] + _dot(p_scr[...], with_ones(vn_ref[pl.ds(last, tk), :])))


def _attn(q, k_new, v_new, k_past, v_past, lam_p, subln_g, lam_init, tile=512):
    B, T, W = q.shape
    H = W // LANES
    tq = min(T, tile)
    has_past = k_past is not None
    q_off = k_past.shape[1] if has_past else 0
    if has_past:
        tile = q_off
    assert T % tq == 0 and q_off % tile == 0 and (T == tq if has_past else tq == tile)
    qspec = pl.BlockSpec((None, tq, LANES), lambda b, h, i: (b, i, h))

    def kspec(a):
        return pl.BlockSpec((None, a.shape[1], LANES), lambda b, h, i: (b, 0, h))

    kv = [k_past, v_past, k_new, v_new] if has_past else [k_new, v_new]
    scratch = [pltpu.VMEM((2 * tq, LANES), F32), pltpu.VMEM((2 * tq, 2 * LANES), F32),
               pltpu.VMEM((2 * tq, tile), BF16)]
    if not has_past:
        scratch.append(pltpu.VMEM((2, 2 * tq, tile), F32))
    return pl.pallas_call(
        functools.partial(_attn_kernel, q_off=q_off, tk=tile, lam_init=lam_init, has_past=has_past),
        grid=(B, H, T // tq),
        in_specs=[qspec] + [kspec(a) for a in kv] +
                 [pl.BlockSpec(lam_p.shape, lambda b, h, i: (0, 0)), pl.BlockSpec((1, LANES), lambda b, h, i: (0, 0))],
        out_specs=qspec,
        out_shape=jax.ShapeDtypeStruct((B, T, W), F32),
        scratch_shapes=scratch,
        compiler_params=_cparams(("parallel", "parallel", "arbitrary")),
        name="attn",
    )(q, *kv, lam_p, subln_g)


def _trunk(x, mods, kv_mod, past_len, wkv0, shift0, k_past, v_past, P):
    B, T, D = x.shape
    pos = past_len + jnp.arange(T)

    def chunks(m, n):
        return [m[:, None, j * D:(j + 1) * D] for j in range(n)]

    row = lambda a: a.reshape(1, -1)
    for l in range(2):
        s1, c1, g1, s2, c2, g2, s3, c3, g3 = chunks(mods[l], 9)
        ng = P['norm_g'][l]
        if l == 1:
            kv_shift, kv_scale = chunks(kv_mod, 2)
            k_new, v_new, kb, vb = _nmm_rope(_nmm_kv_kernel, x, kv_shift, kv_scale, row(P['kv_norm_g']),
                                             P['kv_w'], pos, (D, D, D, D), (F32, F32, BF16, BF16))
            kpb = vpb = None
            if k_past is not None:
                kpb = k_past.reshape(B, past_len, D).astype(BF16)
                vpb = v_past.reshape(B, past_len, D).astype(BF16)
        x = _ffn(x, s1, c1, g1, row(ng[0]), row(ng[1]), P['ffn_w_in'], P['ffn_w_out'], l, 0)
        if l == 0:
            r, k, v, a, lw, g, shift = _rwkv_pre(
                x, s2, c2, row(ng[2]), shift0, P['rwkv_mu'], P['rwkv_w_rkv'], row(P['rwkv_w0']),
                P['rwkv_w1'], P['rwkv_w2'], row(P['rwkv_a0']), P['rwkv_a1'], P['rwkv_a2'],
                P['rwkv_g1'], P['rwkv_g2'])
            o, s_bd = _wkv(r, k, v, a, lw, g, wkv0, row(P['rwkv_k_k']),
                           row(P['rwkv_k_a']), row(P['rwkv_r_k']), row(P['rwkv_ln_w']), row(P['rwkv_ln_b']))
            wkv = _state_from_blockdiag(s_bd)
            w_o = P['rwkv_w_o']
        else:
            (q,) = _nmm_rope(_nmm_q_kernel, x, s2, c2, row(ng[2]), P['diff_w_q'], pos, (D,), (BF16,))
            lam_init = 0.8 - 0.6 * math.exp(-0.3 * l)
            o = _attn(q, kb, vb, kpb, vpb, P['diff_lambda'], row(P['diff_subln_g']), lam_init)
            w_o = P['diff_w_o']
        x = _ffn(x, s3, c3, g3, row(ng[4]), row(ng[5]), P['ffn_w_in'], P['ffn_w_out'], l, 1,
                 mixer=(o, g2, row(ng[3]), w_o))
    return (x, wkv[None], shift[None],
            k_new.reshape(B, T, N_HEADS_B, 2, HEAD_QK), v_new.reshape(B, T, N_HEADS_B, HEAD_V))


def kernel(x_prompt, x_sample, c_prompt, c_sample, state_wkv, state_shift, cache_k, cache_v, ada_w, ada_b, norm_g, ffn_w_in, ffn_w_out, rwkv_mu, rwkv_w_rkv, rwkv_w0, rwkv_w1, rwkv_w2, rwkv_a0, rwkv_a1, rwkv_a2, rwkv_g1, rwkv_g2, rwkv_k_k, rwkv_k_a, rwkv_r_k, rwkv_ln_w, rwkv_ln_b, rwkv_w_o, kv_ada_w, kv_ada_b, kv_norm_g, kv_w, diff_w_q, diff_lambda, diff_subln_g, diff_w_o):
    Bp, _, D = x_prompt.shape
    Bs = x_sample.shape[0]
    H = D // RWKV_HEAD
    bf = lambda w: w.astype(BF16)
    P = dict(norm_g=norm_g, ffn_w_in=bf(ffn_w_in), ffn_w_out=bf(ffn_w_out),
             rwkv_mu=rwkv_mu[0], rwkv_w_rkv=bf(rwkv_w_rkv[0]), rwkv_w0=rwkv_w0[0], rwkv_w1=bf(rwkv_w1[0]),
             rwkv_w2=bf(rwkv_w2[0]), rwkv_a0=rwkv_a0[0], rwkv_a1=bf(rwkv_a1[0]), rwkv_a2=bf(rwkv_a2[0]),
             rwkv_g1=bf(rwkv_g1[0]), rwkv_g2=bf(rwkv_g2[0]), rwkv_k_k=rwkv_k_k[0], rwkv_k_a=rwkv_k_a[0],
             rwkv_r_k=rwkv_r_k[0], rwkv_ln_w=rwkv_ln_w[0], rwkv_ln_b=rwkv_ln_b[0], rwkv_w_o=bf(rwkv_w_o[0]),
             kv_norm_g=kv_norm_g, kv_w=bf(kv_w), diff_w_q=bf(diff_w_q[0]), diff_lambda=diff_lambda[0],
             diff_subln_g=diff_subln_g[0], diff_w_o=bf(diff_w_o[0]))

    c_all = jnp.concatenate([c_prompt, c_sample], axis=0)
    mods = _ada(c_all, ada_w, ada_b[:, None, :])
    kv_mod = _ada(c_all, kv_ada_w[None], kv_ada_b[None, None, :])[0]

    wkv0 = jnp.zeros((Bp, H, RWKV_HEAD, RWKV_HEAD), F32)
    shift0 = jnp.zeros((Bp, 1, D), F32)
    y_p, wkv_p, shift_p, k_p, v_p = _trunk(
        x_prompt, [mods[0, :Bp], mods[1, :Bp]], kv_mod[:Bp], 0, wkv0, shift0, None, None, P)
    y_s, wkv_s, shift_s, k_s, v_s = _trunk(
        x_sample, [mods[0, Bp:], mods[1, Bp:]], kv_mod[Bp:], cache_k.shape[1],
        state_wkv[0], state_shift[0], cache_k, cache_v, P)
    return (y_p, y_s, wkv_p, shift_p, k_p, v_p, wkv_s, shift_s, k_s, v_s)
```

```python
import functools
import math

import jax
import jax.numpy as jnp
from jax import lax
from jax.experimental import pallas as pl
from jax.experimental.pallas import tpu as pltpu

F32 = jnp.float32
BF16 = jnp.bfloat16

CHUNK = 64
RWKV_HEAD = 64
GN_EPS = 64e-5
N_HEADS_B = 8
HEAD_QK = 64
HEAD_V = 128
ROT_DIM = HEAD_QK // 4
ROPE_THETA = 500000.0
ATTN_SCALE = HEAD_QK ** -0.5
LOG2E = math.log2(math.e)
NORM_EPS = 1e-6
NEG_INF = -1e30

LANES = 128
VMEM_LIMIT = 56 * 1024 * 1024


def _cparams(sem):
    return pltpu.CompilerParams(dimension_semantics=sem, vmem_limit_bytes=VMEM_LIMIT)


def _tile(B, T, rows):
    if T >= rows:
        assert T % rows == 0
        return 1, rows
    bb = min(B, max(1, rows // T))
    while B % bb:
        bb -= 1
    return bb, T


def _rms(x, g):
    return x * lax.rsqrt(jnp.mean(x * x, axis=-1, keepdims=True) + NORM_EPS) * g


def _sigmoid(x):
    return 1.0 / (1.0 + jnp.exp(-x))


def _dot(a, b):
    return jnp.dot(a, b, preferred_element_type=F32)


def _ada_kernel(c_ref, w_ref, b_ref, o_ref):
    c = c_ref[...]
    sc = (c * _sigmoid(c)).astype(BF16)
    o_ref[...] = _dot(sc, w_ref[...].astype(BF16)) + b_ref[...]


def _ada(c, w, b, tn=1024):
    L, D, N = w.shape
    M = c.shape[0]
    return pl.pallas_call(
        _ada_kernel,
        grid=(L, N // tn),
        in_specs=[pl.BlockSpec((M, D), lambda l, j: (0, 0)),
                  pl.BlockSpec((None, D, tn), lambda l, j: (l, 0, j)),
                  pl.BlockSpec((None, 1, tn), lambda l, j: (l, 0, j))],
        out_specs=pl.BlockSpec((None, M, tn), lambda l, j: (l, 0, j)),
        out_shape=jax.ShapeDtypeStruct((L, M, N), F32),
        compiler_params=_cparams(("parallel", "parallel")),
        name="ada",
    )(c, w, b)


def _ffn_kernel(*refs, tf, has_mixer):
    if has_mixer:
        a_ref, mg_ref, mng_ref, mw_ref = refs[:4]
        refs = refs[4:]
    x_ref, s_ref, c_ref, g_ref, ngpre_ref, ngpost_ref, wg_ref, wu_ref, wo_ref, o_ref = refs
    bb, tt, D = x_ref.shape
    x = x_ref[...]
    if has_mixer:
        a = a_ref[...].reshape(bb * tt, a_ref.shape[-1]).astype(BF16)
        x = x + mg_ref[...] * _rms(_dot(a, mw_ref[...]).reshape(bb, tt, D), mng_ref[...])
    h = (_rms(x, ngpre_ref[...]) * (1.0 + c_ref[...]) + s_ref[...]).reshape(bb * tt, D).astype(BF16)
    acc = None
    for j in range(wo_ref.shape[0] // tf):
        cols = slice(j * tf, (j + 1) * tf)
        gate = _dot(h, wg_ref[:, cols])
        up = _dot(h, wu_ref[:, cols])
        act = ((gate * _sigmoid(gate)) * up).astype(BF16)
        part = _dot(act, wo_ref[cols, :])
        acc = part if acc is None else acc + part
    o_ref[...] = x + 0.5 * g_ref[...] * _rms(acc.reshape(bb, tt, D), ngpost_ref[...])


def _ffn(x, s, c, g, ng_pre, ng_post, w_in, w_out, l, m, mixer=None, rows=512, tf=256):
    B, T, D = x.shape
    Fh = w_out.shape[2]
    bb, tt = _tile(B, T, rows)
    xspec = pl.BlockSpec((bb, tt, D), lambda b, i: (b, i, 0))
    mspec = pl.BlockSpec((bb, 1, D), lambda b, i: (b, 0, 0))
    gspec = pl.BlockSpec((1, D), lambda b, i: (0, 0))
    once = pl.Buffered(1)
    in_specs = [xspec, mspec, mspec, mspec, gspec, gspec,
                pl.BlockSpec((None, None, D, Fh), lambda b, i: (l, m, 0, 0), pipeline_mode=once),
                pl.BlockSpec((None, None, D, Fh), lambda b, i: (l, m, 0, 1), pipeline_mode=once),
                pl.BlockSpec((None, None, Fh, D), lambda b, i: (l, m, 0, 0), pipeline_mode=once)]
    args = (x, s, c, g, ng_pre, ng_post, w_in, w_in, w_out)
    if mixer is not None:
        a, mg, mng, mw = mixer
        in_specs = [pl.BlockSpec((bb, tt, a.shape[-1]), lambda b, i: (b, i, 0)), mspec, gspec,
                    pl.BlockSpec(mw.shape, lambda b, i: (0, 0), pipeline_mode=once)] + in_specs
        args = (a, mg, mng, mw) + args
    return pl.pallas_call(
        functools.partial(_ffn_kernel, tf=tf, has_mixer=mixer is not None),
        grid=(B // bb, T // tt),
        in_specs=in_specs,
        out_specs=xspec,
        out_shape=jax.ShapeDtypeStruct((B, T, D), F32),
        compiler_params=_cparams(("parallel", "parallel")),
        name="ffn",
    )(*args)


def _rwkv_pre_kernel(x_ref, s_ref, c_ref, ng_ref, shift_ref, mu_ref, wrkv_ref, w0_ref, w1_ref, w2_ref,
                     a0_ref, a1_ref, a2_ref, g1_ref, g2_ref,
                     r_ref, k_ref, v_ref, a_ref, lw_ref, g_ref, shift_out_ref, prev_scr):
    bb, tt, D = x_ref.shape

    @pl.when(pl.program_id(1) == 0)
    def _():
        prev_scr[...] = shift_ref[...]

    hm = _rms(x_ref[...], ng_ref[...]) * (1.0 + c_ref[...]) + s_ref[...]
    row = lax.broadcasted_iota(jnp.int32, hm.shape, 1)
    xprev = jnp.where(row == 0, prev_scr[...], pltpu.roll(hm, 1, axis=1))
    last = hm[:, tt - 1:tt, :]
    prev_scr[...] = last
    shift_out_ref[...] = last
    dx = xprev - hm

    def mix(j):
        return (hm + dx * mu_ref[j:j + 1, :]).reshape(bb * tt, D).astype(BF16)

    def out(ref, val):
        ref[...] = val.reshape(bb, tt, D)

    out(r_ref, _dot(mix(0), wrkv_ref[0]))
    out(k_ref, _dot(mix(2), wrkv_ref[1]))
    out(v_ref, _dot(mix(3), wrkv_ref[2]))
    z = w0_ref[...] + _dot(jnp.tanh(_dot(mix(1), w1_ref[...])).astype(BF16), w2_ref[...])
    nz = -z
    softplus = jnp.maximum(nz, 0.0) + jnp.log1p(jnp.exp(-jnp.abs(nz)))
    out(lw_ref, -jnp.exp(-softplus - 0.5))
    out(a_ref, _sigmoid(a0_ref[...] + _dot(_dot(mix(4), a1_ref[...]).astype(BF16), a2_ref[...])))
    out(g_ref, _dot(_sigmoid(_dot(mix(5), g1_ref[...])).astype(BF16), g2_ref[...]))


def _rwkv_pre(x, s, c, ng, shift_in, mu, wrkv, w0, w1, w2, a0, a1, a2, g1, g2, rows=512):
    B, T, D = x.shape
    bb, tt = _tile(B, T, rows)
    xspec = pl.BlockSpec((bb, tt, D), lambda b, i: (b, i, 0))
    mspec = pl.BlockSpec((bb, 1, D), lambda b, i: (b, 0, 0))

    def full(a):
        return pl.BlockSpec(a.shape, lambda b, i: (0,) * a.ndim, pipeline_mode=pl.Buffered(1))

    big = jax.ShapeDtypeStruct((B, T, D), F32)
    return pl.pallas_call(
        _rwkv_pre_kernel,
        grid=(B // bb, T // tt),
        in_specs=[xspec, mspec, mspec, full(ng), mspec, full(mu), full(wrkv), full(w0), full(w1), full(w2),
                  full(a0), full(a1), full(a2), full(g1), full(g2)],
        out_specs=[xspec] * 6 + [mspec],
        out_shape=[big] * 6 + [jax.ShapeDtypeStruct((B, 1, D), F32)],
        scratch_shapes=[pltpu.VMEM((bb, 1, D), F32)],
        compiler_params=_cparams(("parallel", "arbitrary")),
        name="rwkv_pre",
    )(x, s, c, ng, shift_in, mu, wrkv, w0, w1, w2, a0, a1, a2, g1, g2)


def _pieces(x, n):
    out = []
    for _ in range(n):
        p = x.astype(BF16)
        out.append(p)
        x = x - p.astype(F32)
    return out


def _pdot(a, b, dims):
    order = max(len(a), len(b))
    acc = None
    for i, ai in enumerate(a):
        for j, bj in enumerate(b):
            if i + j < order:
                t = lax.dot_general(ai, bj, (dims, ((), ())), preferred_element_type=F32)
                acc = t if acc is None else acc + t
    return acc


_NN = ((1,), (0,))
_NT = ((1,), (1,))
_TN = ((0,), (0,))

WKV_P_IN = 1
WKV_P_TRI = 1


def _wkv_kernel(r_ref, k_ref, v_ref, a_ref, lw_ref, g_ref, s0_ref, kk_ref, ka_ref, rk_ref, lnw_ref, lnb_ref,
                o_ref, s_ref):
    C = CHUNK
    L2 = 2 * LANES
    npairs = s_ref.shape[0]

    @pl.when(pl.program_id(1) == 0)
    def _():
        s_ref[...] = jnp.zeros_like(s_ref)
        for p in range(npairs):
            s_ref[p, :RWKV_HEAD, :RWKV_HEAD] = s0_ref[2 * p]
            s_ref[p, RWKV_HEAD:, RWKV_HEAD:] = s0_ref[2 * p + 1]

    ri = lax.broadcasted_iota(jnp.int32, (LANES, L2), 0)
    ci = lax.broadcasted_iota(jnp.int32, (LANES, L2), 1) % LANES
    same_head = (ri // C) == (ci // C)
    tril_incl = same_head & ((ci % C) <= (ri % C))
    tril_strict = same_head & ((ci % C) < (ri % C))
    ri1 = lax.broadcasted_iota(jnp.int32, (LANES, LANES), 0)
    ci1 = lax.broadcasted_iota(jnp.int32, (LANES, LANES), 1)
    head_ones = [((ri1 // C) == (ci1 // C)).astype(F32).astype(BF16)]
    ri2 = lax.broadcasted_iota(jnp.int32, (L2, L2), 0)
    ci2 = lax.broadcasted_iota(jnp.int32, (L2, L2), 1)
    head_ones2 = [((ri2 // C) == (ci2 // C)).astype(F32).astype(BF16)]
    eye = (ri1 == ci1).astype(F32)
    r64 = lax.broadcasted_iota(jnp.int32, (C, C), 0)
    c64 = lax.broadcasted_iota(jnp.int32, (C, C), 1)
    cumsum_mat = [(c64 <= r64).astype(F32).astype(BF16)]
    lane = lax.broadcasted_iota(jnp.int32, (C, LANES), 1)
    first_head = lane < RWKV_HEAD

    def stack(x):
        return jnp.concatenate([jnp.where(first_head, x, 0.0), jnp.where(first_head, 0.0, x)], axis=0)

    def level_mask(b):
        return ((ri1 // (2 * b)) == (ci1 // (2 * b))) & ((ri1 % (2 * b)) >= b) & ((ci1 % (2 * b)) < b)

    pairs = range(npairs)

    def pair(x, p):
        return x[:, p * LANES:(p + 1) * LANES]

    def head_sum(x):
        xs = jnp.concatenate([pair(x, p) for p in pairs], axis=0)
        ys = _pdot(_pieces(xs, 2), head_ones, _NN)
        return jnp.concatenate([ys[p * C:(p + 1) * C] for p in pairs], axis=1)

    def head_sum2(x, y):
        xs = jnp.concatenate([jnp.concatenate([pair(x, p), pair(y, p)], axis=1) for p in pairs], axis=0)
        ys = _pdot(_pieces(xs, 2), head_ones2, _NN)
        return (jnp.concatenate([ys[p * C:(p + 1) * C, :LANES] for p in pairs], axis=1),
                jnp.concatenate([ys[p * C:(p + 1) * C, LANES:] for p in pairs], axis=1))

    def chunk(ci, carry):
        rows = pl.ds(pl.multiple_of(ci * C, C), C)
        r, k, v, a, lw = r_ref[rows, :], k_ref[rows, :], v_ref[rows, :], a_ref[rows, :], lw_ref[rows, :]
        kk = k * kk_ref[...]
        kmod = k * (1.0 + (a - 1.0) * ka_ref[...])
        kk_sq, rk_sum = head_sum2(kk * kk, r * kmod * rk_ref[...])
        kk = kk / jnp.maximum(jnp.sqrt(kk_sq), 1e-12)
        bvec = kk * a
        cum = _pdot(cumsum_mat, _pieces(lw, 3), _NN)
        total = cum[C - 1:C, :]
        inv = jnp.exp(-cum)
        rem = jnp.exp(total - cum)
        decay = jnp.exp(total)
        a_t, r_t = -kk * jnp.exp(cum - lw), r * jnp.exp(cum)
        b_t, k_t = bvec * inv, kmod * inv
        b_h, k_h = bvec * rem, kmod * rem

        def stack2(x, y, p):
            return _pieces(jnp.concatenate([stack(pair(x, p)), stack(pair(y, p))], axis=0), WKV_P_IN)

        ar2 = [stack2(a_t, r_t, p) for p in pairs]
        bk2 = [stack2(b_t, k_t, p) for p in pairs]
        bkh2 = [stack2(b_h, k_h, p) for p in pairs]
        v2 = [stack(pair(v, p)) for p in pairs]

        gram = [_pdot(ar2[p], bk2[p], _NT) for p in pairs]
        lab_lak = [jnp.where(tril_strict, gram[p][:LANES], 0.0) for p in pairs]
        lab = [t[:, :LANES] for t in lab_lak]
        lak = [_pieces(t[:, LANES:], WKV_P_TRI) for t in lab_lak]
        mrbk = [_pieces(jnp.where(tril_incl, gram[p][LANES:], 0.0), WKV_P_TRI) for p in pairs]

        x = [eye + jnp.where(level_mask(1), lab[p], 0.0) for p in pairs]
        for b in (2, 4, 8, 16, 32):
            mask = level_mask(b)
            xp = [_pieces(x[p], WKV_P_TRI) for p in pairs]
            t = [_pdot(xp[p], _pieces(jnp.where(mask, lab[p], 0.0), WKV_P_TRI), _NN) for p in pairs]
            x = [x[p] + _pdot(_pieces(t[p], WKV_P_TRI), xp[p], _NN) for p in pairs]

        s = [s_ref[p] for p in pairs]
        ars = [_pdot(ar2[p], _pieces(s[p], WKV_P_IN), _NT) for p in pairs]
        rhs = [ars[p][:LANES] + _pdot(lak[p], _pieces(v2[p], WKV_P_IN), _NN) for p in pairs]
        u2 = [_pdot(_pieces(x[p], WKV_P_TRI), _pieces(rhs[p], WKV_P_IN), _NN) for p in pairs]
        uv = [_pieces(jnp.concatenate([u2[p], v2[p]], axis=0), WKV_P_IN) for p in pairs]
        y2 = [ars[p][LANES:] + _pdot(mrbk[p], uv[p], _NN) for p in pairs]
        for p in pairs:
            s_ref[p] = s[p] * pair(decay, p) + _pdot(uv[p], bkh2[p], _TN)
        y = jnp.concatenate([t[:C] + t[C:] for t in y2], axis=1)

        mean = head_sum(y) * (1.0 / RWKV_HEAD)
        d = y - mean
        var = head_sum(d * d) * (1.0 / RWKV_HEAD)
        yn = d * lax.rsqrt(var + GN_EPS) * lnw_ref[...] + lnb_ref[...]
        bonus = rk_sum * v
        o_ref[rows, :] = (yn + bonus) * g_ref[rows, :]
        return carry

    lax.fori_loop(0, r_ref.shape[0] // C, chunk, 0)


def _wkv(r, k, v, a, lw, g, s0, k_k, k_a, r_k, ln_w, ln_b):
    B, T, D = r.shape
    npairs = D // LANES
    rows = min(T, 4 * CHUNK)
    xspec = pl.BlockSpec((None, rows, D), lambda b, i: (b, i, 0))
    sspec = pl.BlockSpec((None, npairs, LANES, LANES), lambda b, i: (b, 0, 0, 0))
    pspec = pl.BlockSpec((1, D), lambda b, i: (0, 0))
    return pl.pallas_call(
        _wkv_kernel,
        grid=(B, T // rows),
        in_specs=[xspec] * 6 + [pl.BlockSpec((None,) + s0.shape[1:], lambda b, i: (b, 0, 0, 0))] + [pspec] * 5,
        out_specs=[xspec, sspec],
        out_shape=[jax.ShapeDtypeStruct((B, T, D), F32),
                   jax.ShapeDtypeStruct((B, npairs, LANES, LANES), F32)],
        compiler_params=_cparams(("parallel", "arbitrary")),
        name="wkv",
    )(r, k, v, a, lw, g, s0, k_k, k_a, r_k, ln_w, ln_b)


def _state_from_blockdiag(s):
    B, P, N2, _ = s.shape
    N = N2 // 2
    return jnp.stack([s[:, :, :N, :N], s[:, :, N:, N:]], axis=2).reshape(B, 2 * P, N, N)


def _rope_angles(pos):
    half = ROT_DIM // 2
    inv = ROPE_THETA ** (-jnp.arange(half, dtype=F32) * 2.0 / ROT_DIM)
    ang = pos.astype(F32)[:, None] * inv[None, :]
    return jnp.concatenate([jnp.cos(ang), jnp.sin(ang)], axis=-1)


def _rope_rows(y, cs):
    half = ROT_DIM // 2
    r = lax.broadcasted_iota(jnp.int32, (ROT_DIM, 3 * LANES), 0)
    l = lax.broadcasted_iota(jnp.int32, (ROT_DIM, 3 * LANES), 1)
    table, d = l // LANES, l % HEAD_QK
    sel_c = (table == 0) & (r < half) & (d < ROT_DIM) & (d % half == r)
    sel_s1 = (table == 1) & (r >= half) & (d < half) & (d == r - half)
    sel_s2 = (table == 2) & (r >= half) & (d >= half) & (d < ROT_DIM) & (d - half == r - half)
    sel = (jnp.where(sel_c | sel_s2, 1.0, 0.0) - jnp.where(sel_s1, 1.0, 0.0)).astype(BF16)
    coef = _pdot(_pieces(cs, 3), [sel], _NN)
    lane = lax.broadcasted_iota(jnp.int32, (1, LANES), 1)
    c = coef[:, :LANES] + jnp.where(lane % HEAD_QK < ROT_DIM, 0.0, 1.0)
    s1, s2 = coef[:, LANES:2 * LANES], coef[:, 2 * LANES:]
    outs = []
    for p in range(y.shape[-1] // LANES):
        xs = y[:, p * LANES:(p + 1) * LANES]
        outs.append(xs * c + pltpu.roll(xs, LANES - half, axis=1) * s1 + pltpu.roll(xs, half, axis=1) * s2)
    return jnp.concatenate(outs, axis=-1)


def _nmm_kv_kernel(x_ref, s_ref, c_ref, ng_ref, w_ref, cs_ref, k_ref, v_ref, kb_ref, vb_ref):
    bb, tt, D = x_ref.shape
    h = _rms(x_ref[...], ng_ref[...]) * (1.0 + c_ref[...]) + s_ref[...]
    kv = _dot(h.reshape(bb * tt, D).astype(BF16), w_ref[...])
    nk = k_ref.shape[-1]
    k = _rope_rows(kv[:, :nk], cs_ref[...]).reshape(bb, tt, nk)
    v = kv[:, nk:].reshape(bb, tt, v_ref.shape[-1])
    k_ref[...] = k
    v_ref[...] = v
    kb_ref[...] = k.astype(BF16)
    vb_ref[...] = v.astype(BF16)


def _nmm_q_kernel(x_ref, s_ref, c_ref, ng_ref, w_ref, cs_ref, q_ref):
    bb, tt, D = x_ref.shape
    h = _rms(x_ref[...], ng_ref[...]) * (1.0 + c_ref[...]) + s_ref[...]
    q = _dot(h.reshape(bb * tt, D).astype(BF16), w_ref[...])
    q = _rope_rows(q, cs_ref[...]) * (ATTN_SCALE * LOG2E)
    q_ref[...] = q.reshape(bb, tt, q_ref.shape[-1]).astype(BF16)


def _nmm_rope(body, x, s, c, ng, w, pos, out_widths, out_dtypes, rows=512):
    B, T, D = x.shape
    bb, tt = _tile(B, T, rows)
    cs = _rope_angles(pos)
    if bb > 1:
        cs = jnp.tile(cs, (bb, 1))
        tspec = pl.BlockSpec((bb * tt, ROT_DIM), lambda b, i: (0, 0))
    else:
        tspec = pl.BlockSpec((tt, ROT_DIM), lambda b, i: (i, 0))
    xspec = pl.BlockSpec((bb, tt, D), lambda b, i: (b, i, 0))
    mspec = pl.BlockSpec((bb, 1, D), lambda b, i: (b, 0, 0))
    return pl.pallas_call(
        body,
        grid=(B // bb, T // tt),
        in_specs=[xspec, mspec, mspec, pl.BlockSpec((1, D), lambda b, i: (0, 0)),
                  pl.BlockSpec(w.shape, lambda b, i: (0, 0)), tspec],
        out_specs=[pl.BlockSpec((bb, tt, n), lambda b, i: (b, i, 0)) for n in out_widths],
        out_shape=[jax.ShapeDtypeStruct((B, T, n), dt) for n, dt in zip(out_widths, out_dtypes)],
        compiler_params=_cparams(("parallel", "parallel")),
        name="nmm_rope",
    )(x, s, c, ng, w, cs)


def _attn_kernel(*refs, q_off, tk, lam_init, has_past):
    if has_past:
        q_ref, kp_ref, vp_ref, kn_ref, vn_ref, lam_ref, sg_ref, o_ref, m_scr, acc_scr, p_scr = refs
    else:
        q_ref, kn_ref, vn_ref, lam_ref, sg_ref, o_ref, m_scr, acc_scr, p_scr, s_scr = refs
    tq = q_ref.shape[0]
    qi = pl.program_id(2)
    n_full = (q_off + qi * tq) // tk
    lane = lax.broadcasted_iota(jnp.int32, (tq, LANES), 1)
    q = q_ref[...]
    zero = jnp.zeros_like(q)
    qs = jnp.concatenate([jnp.where(lane < HEAD_QK, q, zero), jnp.where(lane < HEAD_QK, zero, q)], axis=0)
    own = pl.multiple_of(qi * tq, CHUNK)

    m_scr[...] = jnp.full_like(m_scr, NEG_INF)
    acc_scr[...] = jnp.zeros_like(acc_scr)

    def with_ones(v):
        return jnp.concatenate([v, jnp.ones_like(v)], axis=1)

    def scores(k_ref, start, size):
        return lax.dot_general(qs, k_ref[pl.ds(start, size), :], (((1,), (1,)), ((), ())),
                               preferred_element_type=F32)

    def softmax_step(s, size):
        m_old = m_scr[...]
        m_new = jnp.maximum(m_old, jnp.max(s, axis=-1, keepdims=True))
        m_wide = m_new[:, :size] if size < LANES else jnp.concatenate([m_new] * (size // LANES), axis=1)
        p = jnp.exp2(s - m_wide).astype(BF16)
        alpha = jnp.exp2(m_old - m_new)
        m_scr[...] = m_new
        return p, jnp.concatenate([alpha, alpha], axis=1)

    def finish(acc):
        lp = lam_ref[...]
        lam = (jnp.exp(jnp.sum(lp[0:1] * lp[1:2], axis=-1, keepdims=True))
               - jnp.exp(jnp.sum(lp[2:3] * lp[3:4], axis=-1, keepdims=True)) + lam_init)
        o = acc[:, :LANES] / acc[:, LANES:]
        o = o[:tq] - lam * o[tq:]
        o = o * lax.rsqrt(jnp.mean(o * o, axis=-1, keepdims=True) + NORM_EPS)
        o_ref[...] = o * sg_ref[...] * (1.0 - lam_init)

    def own_mask():
        row = lax.broadcasted_iota(jnp.int32, (2 * tq, tq), 0)
        col = lax.broadcasted_iota(jnp.int32, (2 * tq, tq), 1)
        return (col // CHUNK) <= ((row % tq) // CHUNK)

    if has_past:
        def full_tile(j, carry):
            start = pl.multiple_of(j * tk, tk)
            p, alpha = softmax_step(scores(kp_ref, start, tk), tk)
            acc_scr[...] = alpha * acc_scr[...] + _dot(p, with_ones(vp_ref[pl.ds(start, tk), :]))
            return carry

        lax.fori_loop(0, n_full, full_tile, 0)
        s = scores(kn_ref, own, tq)
        p, alpha = softmax_step(jnp.where(own_mask(), s, NEG_INF), tq)
        finish(alpha * acc_scr[...] + _dot(p, with_ones(vn_ref[pl.ds(own, tq), :])))
        return

    def tile_start(t):
        return pl.multiple_of(jnp.minimum(t, jnp.maximum(n_full - 1, 0)) * tk, tk)

    def step(t, slot):
        s_next = scores(kn_ref, tile_start(t + 1), tk)
        prev = pl.multiple_of(jnp.where(t == 0, n_full, t - 1) * tk, tk)
        pv = _dot(p_scr[...], with_ones(vn_ref[pl.ds(prev, tk), :]))
        p, alpha = softmax_step(s_scr[slot], tk)
        acc_scr[...] = alpha * (acc_scr[...] + pv)
        p_scr[...] = p
        s_scr[1 - slot] = s_next

    s_own = scores(kn_ref, own, tq)
    s_scr[0] = scores(kn_ref, tile_start(0), tk)
    p, _ = softmax_step(jnp.where(own_mask(), s_own, NEG_INF), tq)
    p_scr[...] = p

    def two_tiles(jj, carry):
        step(2 * jj, 0)

        @pl.when(jj >= 0)
        def _():
            step(2 * jj + 1, 1)

        return carry

    lax.fori_loop(0, n_full // 2, two_tiles, 0)

    @pl.when(n_full % 2 == 1)
    def _():
        step(n_full - 1, 0)

    last = pl.multiple_of(jnp.where(n_full == 0, n_full, n_full - 1) * tk, tk)
    finish(acc_scr[...] + _dot(p_scr[...], with_ones(vn_ref[pl.ds(last, tk), :])))


def _attn(q, k_new, v_new, k_past, v_past, lam_p, subln_g, lam_init, tile=512):
    B, T, W = q.shape
    H = W // LANES
    tq = min(T, tile)
    has_past = k_past is not None
    q_off = k_past.shape[1] if has_past else 0
    if has_past:
        tile = q_off
    assert T % tq == 0 and q_off % tile == 0 and (T == tq if has_past else tq == tile)
    qspec = pl.BlockSpec((None, tq, LANES), lambda b, h, i: (b, i, h))

    def kspec(a):
        return pl.BlockSpec((None, a.shape[1], LANES), lambda b, h, i: (b, 0, h))

    kv = [k_past, v_past, k_new, v_new] if has_past else [k_new, v_new]
    scratch = [pltpu.VMEM((2 * tq, LANES), F32), pltpu.VMEM((2 * tq, 2 * LANES), F32),
               pltpu.VMEM((2 * tq, tile), BF16)]
    if not has_past:
        scratch.append(pltpu.VMEM((2, 2 * tq, tile), F32))
    return pl.pallas_call(
        functools.partial(_attn_kernel, q_off=q_off, tk=tile, lam_init=lam_init, has_past=has_past),
        grid=(B, H, T // tq),
        in_specs=[qspec] + [kspec(a) for a in kv] +
                 [pl.BlockSpec(lam_p.shape, lambda b, h, i: (0, 0)), pl.BlockSpec((1, LANES), lambda b, h, i: (0, 0))],
        out_specs=qspec,
        out_shape=jax.ShapeDtypeStruct((B, T, W), F32),
        scratch_shapes=scratch,
        compiler_params=_cparams(("parallel", "parallel", "arbitrary")),
        name="attn",
    )(q, *kv, lam_p, subln_g)


def _trunk(x, mods, kv_mod, past_len, wkv0, shift0, k_past, v_past, P):
    B, T, D = x.shape
    pos = past_len + jnp.arange(T)

    def chunks(m, n):
        return [m[:, None, j * D:(j + 1) * D] for j in range(n)]

    row = lambda a: a.reshape(1, -1)
    for l in range(2):
        s1, c1, g1, s2, c2, g2, s3, c3, g3 = chunks(mods[l], 9)
        ng = P['norm_g'][l]
        if l == 1:
            kv_shift, kv_scale = chunks(kv_mod, 2)
            k_new, v_new, kb, vb = _nmm_rope(_nmm_kv_kernel, x, kv_shift, kv_scale, row(P['kv_norm_g']),
                                             P['kv_w'], pos, (D, D, D, D), (F32, F32, BF16, BF16))
            kpb = vpb = None
            if k_past is not None:
                kpb = k_past.reshape(B, past_len, D).astype(BF16)
                vpb = v_past.reshape(B, past_len, D).astype(BF16)
        x = _ffn(x, s1, c1, g1, row(ng[0]), row(ng[1]), P['ffn_w_in'], P['ffn_w_out'], l, 0)
        if l == 0:
            r, k, v, a, lw, g, shift = _rwkv_pre(
                x, s2, c2, row(ng[2]), shift0, P['rwkv_mu'], P['rwkv_w_rkv'], row(P['rwkv_w0']),
                P['rwkv_w1'], P['rwkv_w2'], row(P['rwkv_a0']), P['rwkv_a1'], P['rwkv_a2'],
                P['rwkv_g1'], P['rwkv_g2'])
            o, s_bd = _wkv(r, k, v, a, lw, g, wkv0, row(P['rwkv_k_k']),
                           row(P['rwkv_k_a']), row(P['rwkv_r_k']), row(P['rwkv_ln_w']), row(P['rwkv_ln_b']))
            wkv = _state_from_blockdiag(s_bd)
            w_o = P['rwkv_w_o']
        else:
            (q,) = _nmm_rope(_nmm_q_kernel, x, s2, c2, row(ng[2]), P['diff_w_q'], pos, (D,), (BF16,))
            lam_init = 0.8 - 0.6 * math.exp(-0.3 * l)
            o = _attn(q, kb, vb, kpb, vpb, P['diff_lambda'], row(P['diff_subln_g']), lam_init)
            w_o = P['diff_w_o']
        x = _ffn(x, s3, c3, g3, row(ng[4]), row(ng[5]), P['ffn_w_in'], P['ffn_w_out'], l, 1,
                 mixer=(o, g2, row(ng[3]), w_o))
    return (x, wkv[None], shift[None],
            k_new.reshape(B, T, N_HEADS_B, 2, HEAD_QK), v_new.reshape(B, T, N_HEADS_B, HEAD_V))


def kernel(x_prompt, x_sample, c_prompt, c_sample, state_wkv, state_shift, cache_k, cache_v, ada_w, ada_b, norm_g, ffn_w_in, ffn_w_out, rwkv_mu, rwkv_w_rkv, rwkv_w0, rwkv_w1, rwkv_w2, rwkv_a0, rwkv_a1, rwkv_a2, rwkv_g1, rwkv_g2, rwkv_k_k, rwkv_k_a, rwkv_r_k, rwkv_ln_w, rwkv_ln_b, rwkv_w_o, kv_ada_w, kv_ada_b, kv_norm_g, kv_w, diff_w_q, diff_lambda, diff_subln_g, diff_w_o):
    Bp, _, D = x_prompt.shape
    Bs = x_sample.shape[0]
    H = D // RWKV_HEAD
    bf = lambda w: w.astype(BF16)
    P = dict(norm_g=norm_g, ffn_w_in=bf(ffn_w_in), ffn_w_out=bf(ffn_w_out),
             rwkv_mu=rwkv_mu[0], rwkv_w_rkv=bf(rwkv_w_rkv[0]), rwkv_w0=rwkv_w0[0], rwkv_w1=bf(rwkv_w1[0]),
             rwkv_w2=bf(rwkv_w2[0]), rwkv_a0=rwkv_a0[0], rwkv_a1=bf(rwkv_a1[0]), rwkv_a2=bf(rwkv_a2[0]),
             rwkv_g1=bf(rwkv_g1[0]), rwkv_g2=bf(rwkv_g2[0]), rwkv_k_k=rwkv_k_k[0], rwkv_k_a=rwkv_k_a[0],
             rwkv_r_k=rwkv_r_k[0], rwkv_ln_w=rwkv_ln_w[0], rwkv_ln_b=rwkv_ln_b[0], rwkv_w_o=bf(rwkv_w_o[0]),
             kv_norm_g=kv_norm_g, kv_w=bf(kv_w), diff_w_q=bf(diff_w_q[0]), diff_lambda=diff_lambda[0],
             diff_subln_g=diff_subln_g[0], diff_w_o=bf(diff_w_o[0]))

    c_all = jnp.concatenate([c_prompt, c_sample], axis=0)
    mods = _ada(c_all, ada_w, ada_b[:, None, :])
    kv_mod = _ada(c_all, kv_ada_w[None], kv_ada_b[None, None, :])[0]

    wkv0 = jnp.zeros((Bp, H, RWKV_HEAD, RWKV_HEAD), F32)
    shift0 = jnp.zeros((Bp, 1, D), F32)
    y_p, wkv_p, shift_p, k_p, v_p = _trunk(
        x_prompt, [mods[0, :Bp], mods[1, :Bp]], kv_mod[:Bp], 0, wkv0, shift0, None, None, P)
    y_s, wkv_s, shift_s, k_s, v_s = _trunk(
        x_sample, [mods[0, Bp:], mods[1, Bp:]], kv_mod[Bp:], cache_k.shape[1],
        state_wkv[0], state_shift[0], cache_k, cache_v, P)
    return (y_p, y_s, wkv_p, shift_p, k_p, v_p, wkv_s, shift_s, k_s, v_s)
```
